```python
import jax, jax.numpy as jnp
from jax import lax
import numpy as np

D_MODEL = 1024
BATCH = 16
SEQ = 2048
DEPTH = 4

N_MIXERS = 2
N_CONV_LAYERS = (DEPTH + 1) // 2
N_MLA_LAYERS = DEPTH // 2
N_DENSE_LAYERS = (DEPTH + 1) // 2
N_MOE_LAYERS = DEPTH // 2
CONV_WIDTH = 3
N_HEADS = 8
QK_NOPE_DIM = 128
QK_ROPE_DIM = 64
QK_HEAD_DIM = QK_NOPE_DIM + QK_ROPE_DIM
V_HEAD_DIM = 128
Q_LORA_RANK = 384
KV_LORA_RANK = 256
ROPE_THETA = 10000.0
Q_BLOCK = 128
D_FF_DENSE = 2816
N_EXPERTS = 8
TOP_K = 2
D_FF_EXPERT = 2048
EPS = 1e-6

kernel_name = "hybrid_shortconv_mla_moe_trunk"


def rmsnorm(x, g):
    xf = x.astype(jnp.float32)
    y = xf * lax.rsqrt(jnp.mean(xf * xf, axis=-1, keepdims=True) + EPS)
    return (y * g.astype(jnp.float32)).astype(x.dtype)


def rope_tables(positions):
    inv_freq = ROPE_THETA ** (-jnp.arange(0, QK_ROPE_DIM, 2, dtype=jnp.float32) / QK_ROPE_DIM)
    ang = positions.astype(jnp.float32)[..., None] * inv_freq
    return jnp.cos(ang)[:, :, None, :], jnp.sin(ang)[:, :, None, :]


def apply_rope(x, cos, sin):
    x1, x2 = jnp.split(x, 2, axis=-1)
    cos = cos.astype(x.dtype)
    sin = sin.astype(x.dtype)
    return jnp.concatenate([x1 * cos - x2 * sin, x2 * cos + x1 * sin], axis=-1)


def short_conv_mixer(h, w_in, w_conv, w_out):
    seq = h.shape[1]
    b_gate, c_gate, u = jnp.split(h @ w_in, 3, axis=-1)
    v = c_gate * u
    vp = jnp.pad(v, ((0, 0), (CONV_WIDTH - 1, 0), (0, 0)))
    conv = vp[:, 0:seq] * w_conv[0]
    for k in range(1, CONV_WIDTH):
        conv = conv + vp[:, k:k + seq] * w_conv[k]
    return (b_gate * conv) @ w_out


def mla_mixer(h, cos, sin, w_down, q_a_norm, kv_a_norm, w_uq, w_ukv, q_norm, k_norm, w_o):
    bsz, seq, _ = h.shape
    down = h @ w_down
    c_q = rmsnorm(down[..., :Q_LORA_RANK], q_a_norm)
    c_kv = rmsnorm(down[..., Q_LORA_RANK:Q_LORA_RANK + KV_LORA_RANK], kv_a_norm)
    k_rope = down[..., Q_LORA_RANK + KV_LORA_RANK:]
    q = (c_q @ w_uq).reshape(bsz, seq, N_HEADS, QK_HEAD_DIM)
    kv = (c_kv @ w_ukv).reshape(bsz, seq, N_HEADS, QK_NOPE_DIM + V_HEAD_DIM)
    k_nope, v = kv[..., :QK_NOPE_DIM], kv[..., QK_NOPE_DIM:]
    k_rope_h = jnp.broadcast_to(k_rope[:, :, None, :], (bsz, seq, N_HEADS, QK_ROPE_DIM))
    k = jnp.concatenate([k_nope, k_rope_h], axis=-1)
    q = rmsnorm(q, q_norm)
    k = rmsnorm(k, k_norm)
    q = jnp.concatenate([q[..., :QK_NOPE_DIM], apply_rope(q[..., QK_NOPE_DIM:], cos, sin)], axis=-1)
    k = jnp.concatenate([k[..., :QK_NOPE_DIM], apply_rope(k[..., QK_NOPE_DIM:], cos, sin)], axis=-1)
    scale = QK_HEAD_DIM ** -0.5
    neg = jnp.finfo(jnp.float32).min
    outs = []
    for blk in range(seq // Q_BLOCK):
        s0 = blk * Q_BLOCK
        end = s0 + Q_BLOCK
        scores = jnp.einsum('bqhd,bkhd->bhqk', q[:, s0:end], k[:, :end]).astype(jnp.float32) * scale
        mask = jnp.arange(end)[None, :] <= (s0 + jnp.arange(Q_BLOCK))[:, None]
        p = jax.nn.softmax(jnp.where(mask, scores, neg), axis=-1).astype(v.dtype)
        outs.append(jnp.einsum('bhqk,bkhd->bqhd', p, v[:, :end]))
    o = jnp.concatenate(outs, axis=1).reshape(bsz, seq, N_HEADS * V_HEAD_DIM)
    return o @ w_o


def swiglu(h, w_gate, w_up, w_down):
    return (jax.nn.silu(h @ w_gate) * (h @ w_up)) @ w_down


def moe_swiglu(h, router, w_gate, w_up, w_down):
    bsz, seq, d = h.shape
    ht = h.reshape(bsz * seq, d)
    probs = jax.nn.softmax((ht @ router).astype(jnp.float32), axis=-1)
    vals, idx = lax.top_k(probs, TOP_K)
    wts = vals / jnp.sum(vals, axis=-1, keepdims=True)
    gates = jnp.einsum('tk,tke->te', wts, jax.nn.one_hot(idx, N_EXPERTS, dtype=jnp.float32)).astype(h.dtype)
    out = jnp.zeros_like(ht)
    for e in range(N_EXPERTS):
        out = out + gates[:, e:e + 1] * swiglu(ht, w_gate[e], w_up[e], w_down[e])
    return out.reshape(bsz, seq, d)


def setup_inputs(seed: int = 0) -> dict:
    key = jax.random.key(seed)
    ks = jax.random.split(key, 24)
    f32 = jnp.float32
    res_scale = (2 * DEPTH) ** -0.5

    def w(k, shape, fan_in, extra=1.0):
        return jax.random.normal(k, shape, f32) * (fan_in ** -0.5) * extra

    def gain(k, shape):
        return 1.0 + 0.02 * jax.random.normal(k, shape, f32)

    x = jax.random.normal(ks[0], (BATCH, SEQ, D_MODEL), f32)
    start = jax.random.randint(ks[1], (BATCH, 1), 0, 4096, dtype=jnp.int32)
    positions = (start + jnp.arange(SEQ, dtype=jnp.int32)[None, :]).astype(jnp.int32)
    return {
        "x": x,
        "positions": positions,
        "norm_mix": gain(ks[2], (DEPTH, D_MODEL)),
        "norm_ffn": gain(ks[3], (DEPTH, D_MODEL)),
        "conv_w_in": w(ks[4], (N_CONV_LAYERS, D_MODEL, 3 * D_MODEL), D_MODEL),
        "conv_w": w(ks[5], (N_CONV_LAYERS, CONV_WIDTH, D_MODEL), CONV_WIDTH),
        "conv_w_out": w(ks[6], (N_CONV_LAYERS, D_MODEL, D_MODEL), D_MODEL, res_scale),
        "mla_w_down": w(ks[7], (N_MLA_LAYERS, D_MODEL, Q_LORA_RANK + KV_LORA_RANK + QK_ROPE_DIM), D_MODEL),
        "mla_q_a_norm": gain(ks[8], (N_MLA_LAYERS, Q_LORA_RANK)),
        "mla_kv_a_norm": gain(ks[9], (N_MLA_LAYERS, KV_LORA_RANK)),
        "mla_w_uq": w(ks[10], (N_MLA_LAYERS, Q_LORA_RANK, N_HEADS * QK_HEAD_DIM), Q_LORA_RANK),
        "mla_w_ukv": w(ks[11], (N_MLA_LAYERS, KV_LORA_RANK, N_HEADS * (QK_NOPE_DIM + V_HEAD_DIM)), KV_LORA_RANK),
        "mla_q_norm": gain(ks[12], (N_MLA_LAYERS, QK_HEAD_DIM)),
        "mla_k_norm": gain(ks[13], (N_MLA_LAYERS, QK_HEAD_DIM)),
        "mla_w_o": w(ks[14], (N_MLA_LAYERS, N_HEADS * V_HEAD_DIM, D_MODEL), N_HEADS * V_HEAD_DIM, res_scale),
        "ffn_w_gate": w(ks[15], (N_DENSE_LAYERS, D_MODEL, D_FF_DENSE), D_MODEL),
        "ffn_w_up": w(ks[16], (N_DENSE_LAYERS, D_MODEL, D_FF_DENSE), D_MODEL),
        "ffn_w_down": w(ks[17], (N_DENSE_LAYERS, D_FF_DENSE, D_MODEL), D_FF_DENSE, res_scale),
        "moe_router": w(ks[18], (N_MOE_LAYERS, D_MODEL, N_EXPERTS), D_MODEL),
        "moe_w_gate": w(ks[19], (N_MOE_LAYERS, N_EXPERTS, D_MODEL, D_FF_EXPERT), D_MODEL),
        "moe_w_up": w(ks[20], (N_MOE_LAYERS, N_EXPERTS, D_MODEL, D_FF_EXPERT), D_MODEL),
        "moe_w_down": w(ks[21], (N_MOE_LAYERS, N_EXPERTS, D_FF_EXPERT, D_MODEL), D_FF_EXPERT, res_scale),
    }


def reference(x, positions, norm_mix, norm_ffn, conv_w_in, conv_w, conv_w_out,
              mla_w_down, mla_q_a_norm, mla_kv_a_norm, mla_w_uq, mla_w_ukv,
              mla_q_norm, mla_k_norm, mla_w_o, ffn_w_gate, ffn_w_up, ffn_w_down,
              moe_router, moe_w_gate, moe_w_up, moe_w_down):
    cos, sin = rope_tables(positions)
    for i in range(DEPTH):
        j = i // 2
        h = rmsnorm(x, norm_mix[i])
        if i % N_MIXERS == 0:
            x = x + short_conv_mixer(h, conv_w_in[j], conv_w[j], conv_w_out[j])
        else:
            x = x + mla_mixer(h, cos, sin, mla_w_down[j], mla_q_a_norm[j], mla_kv_a_norm[j],
                              mla_w_uq[j], mla_w_ukv[j], mla_q_norm[j], mla_k_norm[j], mla_w_o[j])
        h = rmsnorm(x, norm_ffn[i])
        if i % 2 == 0:
            x = x + swiglu(h, ffn_w_gate[j], ffn_w_up[j], ffn_w_down[j])
        else:
            x = x + moe_swiglu(h, moe_router[j], moe_w_gate[j], moe_w_up[j], moe_w_down[j])
    return x
```

```python
import functools

import jax
import jax.numpy as jnp
from jax import lax
from jax.experimental import pallas as pl
from jax.experimental.pallas import tpu as pltpu

N_HEADS = 8
QK_NOPE_DIM = 128
QK_ROPE_DIM = 64
QK_HEAD_DIM = QK_NOPE_DIM + QK_ROPE_DIM
V_HEAD_DIM = 128
Q_LORA_RANK = 384
KV_LORA_RANK = 256
ROPE_THETA = 10000.0
N_EXPERTS = 8
TOP_K = 2
EPS = 1e-6

V7X_LANES = 128
V7X_SUBLANES = 8
V7X_VMEM_BYTES = 64 * 1024 * 1024
BIG_VMEM_LIMIT = V7X_VMEM_BYTES - 8 * 1024 * 1024

QK_PAD_DIM = 2 * V7X_LANES
MASK_VALUE = -1e30

ROW_TILE = 512
ATTN_Q_TILE = 512
ATTN_K_TILE = 512
GROUP_TILE = 512
MOVE_TILE = 256

F32 = jnp.float32
BF16 = jnp.bfloat16


def _rms(x, gain):
    return x * lax.rsqrt(jnp.mean(x * x, axis=-1, keepdims=True) + EPS) * gain


def _dot(a, b):
    return jnp.dot(a, b, preferred_element_type=F32)


def _resident(shape):
    nd = len(shape)
    return pl.BlockSpec(shape, lambda *_: (0,) * nd, pipeline_mode=pl.Buffered(1))


def _conv_kernel(x_ref, g_ref, w_in_ref, cw_ref, w_out_ref, o_ref, carry_ref):
    d = x_ref.shape[-1]
    tm = x_ref.shape[1]

    @pl.when(pl.program_id(1) == 0)
    def _():
        carry_ref[...] = jnp.zeros_like(carry_ref)

    x = x_ref[0]
    h = _rms(x, g_ref[...]).astype(BF16)
    y = _dot(h, w_in_ref[...])
    b_gate = y[:, :d]
    v = y[:, d:2 * d] * y[:, 2 * d:]
    prev = carry_ref[...]
    p1 = prev[V7X_SUBLANES - 1:V7X_SUBLANES]
    p2 = prev[V7X_SUBLANES - 2:V7X_SUBLANES - 1]
    rows = lax.broadcasted_iota(jnp.int32, (tm, d), 0)
    v1 = jnp.where(rows == 0, p1, pltpu.roll(v, 1, 0))
    v2 = jnp.where(rows == 0, p2, jnp.where(rows == 1, p1, pltpu.roll(v, 2, 0)))
    carry_ref[...] = v[tm - V7X_SUBLANES:]
    cw = cw_ref[...]
    conv = v2 * cw[0:1] + v1 * cw[1:2] + v * cw[2:3]
    z = (b_gate * conv).astype(BF16)
    o_ref[0] = x + _dot(z, w_out_ref[...])


def _conv_mixer(x, gain, w_in, conv_w, w_out):
    bsz, seq, d = x.shape
    tm = min(ROW_TILE, seq)
    return pl.pallas_call(
        _conv_kernel,
        grid=(bsz, seq // tm),
        in_specs=[
            pl.BlockSpec((1, tm, d), lambda b, s: (b, s, 0)),
            _resident((1, d)),
            _resident(w_in.shape),
            _resident(conv_w.shape),
            _resident(w_out.shape),
        ],
        out_specs=pl.BlockSpec((1, tm, d), lambda b, s: (b, s, 0)),
        out_shape=jax.ShapeDtypeStruct(x.shape, F32),
        scratch_shapes=[pltpu.VMEM((V7X_SUBLANES, d), F32)],
        compiler_params=pltpu.CompilerParams(
            dimension_semantics=("arbitrary", "arbitrary"), vmem_limit_bytes=BIG_VMEM_LIMIT),
        name="conv_mixer",
    )(x, gain.reshape(1, d), w_in, conv_w, w_out)


def _ffn_kernel(x_ref, g_ref, wg_ref, wu_ref, wd_ref, o_ref):
    x = x_ref[...]
    h = _rms(x, g_ref[...]).astype(BF16)
    gate = _dot(h, wg_ref[...])
    up = _dot(h, wu_ref[...])
    act = (gate * jax.nn.sigmoid(gate) * up).astype(BF16)
    o_ref[...] = x + _dot(act, wd_ref[...])


def _dense_ffn(x2, gain, wg, wu, wd):
    t, d = x2.shape
    tm = min(ROW_TILE, t)
    return pl.pallas_call(
        _ffn_kernel,
        grid=(t // tm,),
        in_specs=[
            pl.BlockSpec((tm, d), lambda i: (i, 0)),
            _resident((1, d)),
            _resident(wg.shape),
            _resident(wu.shape),
            _resident(wd.shape),
        ],
        out_specs=pl.BlockSpec((tm, d), lambda i: (i, 0)),
        out_shape=jax.ShapeDtypeStruct(x2.shape, F32),
        compiler_params=pltpu.CompilerParams(
            dimension_semantics=("arbitrary",), vmem_limit_bytes=BIG_VMEM_LIMIT),
        name="dense_ffn",
    )(x2, gain.reshape(1, d), wg, wu, wd)


def _rope_table_kernel(pos_ref, inv_ref, cos_ref, sin_ref, nsin_ref):
    ang = pos_ref[...] * inv_ref[...]
    s = jnp.sin(ang)
    cos_ref[...] = jnp.cos(ang)
    sin_ref[...] = s
    nsin_ref[...] = -s


def _rope_tables(positions):
    bsz, seq = positions.shape
    half = QK_ROPE_DIM // 2
    t = bsz * seq
    inv_freq = ROPE_THETA ** (-jnp.arange(0, QK_ROPE_DIM, 2, dtype=F32) / QK_ROPE_DIM)
    per_row = V7X_LANES // half
    rows = t // per_row
    pos_rep = jnp.repeat(positions.reshape(-1).astype(F32), half).reshape(rows, V7X_LANES)
    inv_rep = jnp.tile(inv_freq, per_row).reshape(1, V7X_LANES)
    tr = min(1024, rows)
    shp = jax.ShapeDtypeStruct((rows, V7X_LANES), F32)
    cos, sin, nsin = pl.pallas_call(
        _rope_table_kernel,
        grid=(rows // tr,),
        in_specs=[pl.BlockSpec((tr, V7X_LANES), lambda i: (i, 0)),
                  pl.BlockSpec((1, V7X_LANES), lambda i: (0, 0))],
        out_specs=[pl.BlockSpec((tr, V7X_LANES), lambda i: (i, 0))] * 3,
        out_shape=[shp, shp, shp],
        name="rope_tables",
    )(pos_rep, inv_rep)
    cos, sin, nsin = (a.reshape(bsz, seq, half) for a in (cos, sin, nsin))
    zeros = jnp.zeros((bsz, seq, V7X_LANES - QK_ROPE_DIM), F32)
    cos_t = jnp.concatenate([cos, cos, zeros], axis=-1)
    sinm_t = jnp.concatenate([nsin, sin, zeros], axis=-1)
    return cos_t, sinm_t


def _mla_proj_kernel(x_ref, g_ref, wdn_ref, qa_ref, kva_ref, wuq_ref, wukv_ref,
                     qn_ref, kn_ref, cos_ref, sinm_ref, q_ref, k_ref, v_ref):
    x = x_ref[0]
    h = _rms(x, g_ref[...]).astype(BF16)
    down = _dot(h, wdn_ref[...])
    c_q = _rms(down[:, :Q_LORA_RANK], qa_ref[...]).astype(BF16)
    c_kv = _rms(down[:, Q_LORA_RANK:Q_LORA_RANK + KV_LORA_RANK], kva_ref[...]).astype(BF16)
    k_rope = down[:, Q_LORA_RANK + KV_LORA_RANK:]
    q = _dot(c_q, wuq_ref[...])
    kv = _dot(c_kv, wukv_ref[...])
    cos = cos_ref[0]
    sinm = sinm_ref[0]
    qn = qn_ref[...]
    kn = kn_ref[...]

    def rope(r):
        return r * cos + (pltpu.roll(r, 32, 1) + pltpu.roll(r, 96, 1)) * sinm

    scale = QK_HEAD_DIM ** -0.5
    kr = k_rope * kn[:, V7X_LANES:]
    kr_ss = jnp.sum(k_rope * k_rope, axis=-1, keepdims=True)
    kr_rot = rope(kr)
    for hd in range(N_HEADS):
        qh = q[:, hd * QK_PAD_DIM:(hd + 1) * QK_PAD_DIM]
        inv_q = lax.rsqrt(jnp.sum(qh * qh, axis=-1, keepdims=True) / QK_HEAD_DIM + EPS) * scale
        qh = qh * inv_q * qn
        q_ref[0, hd, :, :V7X_LANES] = qh[:, :V7X_LANES].astype(BF16)
        q_ref[0, hd, :, V7X_LANES:] = rope(qh[:, V7X_LANES:]).astype(BF16)
        k_nope = kv[:, hd * QK_PAD_DIM:hd * QK_PAD_DIM + QK_NOPE_DIM]
        ss = jnp.sum(k_nope * k_nope, axis=-1, keepdims=True) + kr_ss
        inv_k = lax.rsqrt(ss / QK_HEAD_DIM + EPS)
        k_ref[0, hd, :, :V7X_LANES] = (k_nope * inv_k * kn[:, :V7X_LANES]).astype(BF16)
        k_ref[0, hd, :, V7X_LANES:] = (kr_rot * inv_k).astype(BF16)
        v_ref[0, hd] = kv[:, hd * QK_PAD_DIM + QK_NOPE_DIM:(hd + 1) * QK_PAD_DIM].astype(BF16)


def _pad_head_gain(gain):
    return jnp.concatenate([gain, jnp.zeros((QK_PAD_DIM - QK_HEAD_DIM,), F32)]).reshape(1, QK_PAD_DIM)


def _mla_proj(x, gain, w_down, q_a_norm, kv_a_norm, w_uq, w_ukv, q_norm, k_norm, cos_t, sinm_t):
    bsz, seq, d = x.shape
    tm = min(ROW_TILE, seq)
    n_down = w_down.shape[1]
    pad_down = (-n_down) % (2 * V7X_LANES)
    wdn = jnp.pad(w_down, ((0, 0), (0, pad_down))).astype(BF16)
    wuq = w_uq.reshape(Q_LORA_RANK, N_HEADS, QK_HEAD_DIM)
    wuq = jnp.pad(wuq, ((0, 0), (0, 0), (0, QK_PAD_DIM - QK_HEAD_DIM)))
    wuq = wuq.reshape(Q_LORA_RANK, N_HEADS * QK_PAD_DIM).astype(BF16)
    wukv = w_ukv.astype(BF16)
    row_spec = pl.BlockSpec((1, tm, d), lambda b, s: (b, s, 0))
    tab_spec = pl.BlockSpec((1, tm, V7X_LANES), lambda b, s: (b, s, 0))
    qk_spec = pl.BlockSpec((1, N_HEADS, tm, QK_PAD_DIM), lambda b, s: (b, 0, s, 0))
    v_spec = pl.BlockSpec((1, N_HEADS, tm, V_HEAD_DIM), lambda b, s: (b, 0, s, 0))
    qk_shape = jax.ShapeDtypeStruct((bsz, N_HEADS, seq, QK_PAD_DIM), BF16)
    v_shape = jax.ShapeDtypeStruct((bsz, N_HEADS, seq, V_HEAD_DIM), BF16)
    return pl.pallas_call(
        _mla_proj_kernel,
        grid=(bsz, seq // tm),
        in_specs=[
            row_spec,
            _resident((1, d)),
            _resident(wdn.shape),
            _resident((1, Q_LORA_RANK)),
            _resident((1, KV_LORA_RANK)),
            _resident(wuq.shape),
            _resident(wukv.shape),
            _resident((1, QK_PAD_DIM)),
            _resident((1, QK_PAD_DIM)),
            tab_spec,
            tab_spec,
        ],
        out_specs=[qk_spec, qk_spec, v_spec],
        out_shape=[qk_shape, qk_shape, v_shape],
        compiler_params=pltpu.CompilerParams(
            dimension_semantics=("arbitrary", "arbitrary"), vmem_limit_bytes=BIG_VMEM_LIMIT),
        name="mla_proj",
    )(x, gain.reshape(1, d), wdn, q_a_norm.reshape(1, -1), kv_a_norm.reshape(1, -1), wuq, wukv,
      _pad_head_gain(q_norm), _pad_head_gain(k_norm), cos_t, sinm_t)


def _attn_kernel(q_ref, k_ref, v_ref, o_ref, m_ref, l_ref, acc_ref):
    tq = q_ref.shape[2]
    tk = k_ref.shape[2]
    qi = pl.program_id(2)
    ki = pl.program_id(3)
    last_ki = ((qi + 1) * tq - 1) // tk

    @pl.when(ki == 0)
    def _():
        m_ref[...] = jnp.full_like(m_ref, MASK_VALUE)
        l_ref[...] = jnp.zeros_like(l_ref)
        acc_ref[...] = jnp.zeros_like(acc_ref)

    @pl.when(ki <= last_ki)
    def _():
        s = lax.dot_general(q_ref[0, 0], k_ref[0, 0], (((1,), (1,)), ((), ())),
                            preferred_element_type=F32)
        row = qi * tq + lax.broadcasted_iota(jnp.int32, (tq, tk), 0)
        col = ki * tk + lax.broadcasted_iota(jnp.int32, (tq, tk), 1)
        s = jnp.where(col <= row, s, MASK_VALUE)
        m_prev = m_ref[...]
        m_new = jnp.maximum(m_prev, jnp.max(s, axis=-1, keepdims=True))
        alpha = jnp.exp(m_prev - m_new)
        p = jnp.exp(s - m_new)
        l_ref[...] = alpha * l_ref[...] + jnp.sum(p, axis=-1, keepdims=True)
        acc_ref[...] = alpha * acc_ref[...] + _dot(p.astype(BF16), v_ref[0, 0])
        m_ref[...] = m_new

    @pl.when(ki == last_ki)
    def _():
        o_ref[0] = (acc_ref[...] / l_ref[...]).astype(o_ref.dtype)


def _attention(q, k, v):
    bsz, nh, seq, _ = q.shape
    dv = v.shape[-1]
    tq = min(ATTN_Q_TILE, seq)
    tk = min(ATTN_K_TILE, seq)

    def kv_map(b, h, qi, ki):
        return (b, h, jnp.minimum(ki, ((qi + 1) * tq - 1) // tk), 0)

    return pl.pallas_call(
        _attn_kernel,
        grid=(bsz, nh, seq // tq, seq // tk),
        in_specs=[
            pl.BlockSpec((1, 1, tq, q.shape[-1]), lambda b, h, qi, ki: (b, h, qi, 0)),
            pl.BlockSpec((1, 1, tk, k.shape[-1]), kv_map),
            pl.BlockSpec((1, 1, tk, dv), kv_map),
        ],
        out_specs=pl.BlockSpec((1, tq, dv), lambda b, h, qi, ki: (b, qi, h)),
        out_shape=jax.ShapeDtypeStruct((bsz, seq, nh * dv), BF16),
        scratch_shapes=[pltpu.VMEM((tq, 1), F32), pltpu.VMEM((tq, 1), F32), pltpu.VMEM((tq, dv), F32)],
        compiler_params=pltpu.CompilerParams(
            dimension_semantics=("arbitrary",) * 4),
        name="causal_attention",
    )(q, k, v)


def _oproj_route_kernel(x_ref, o_ref, wo_ref, g_ref, wr_ref, xo_ref, h_ref, idx_ref, gate_ref):
    tm = x_ref.shape[0]
    x = x_ref[...] + _dot(o_ref[...], wo_ref[...])
    xo_ref[...] = x
    hn = _rms(x, g_ref[...])
    h_ref[...] = hn
    logits = _dot(hn.astype(BF16), wr_ref[...])
    lane = lax.broadcasted_iota(jnp.int32, logits.shape, 1)
    logits = jnp.where(lane < N_EXPERTS, logits, MASK_VALUE)
    e = jnp.exp(logits - jnp.max(logits, axis=-1, keepdims=True))
    probs = e / jnp.sum(e, axis=-1, keepdims=True)
    probs = jnp.where(lane < N_EXPERTS, probs, -1.0)
    p1 = jnp.max(probs, axis=-1, keepdims=True)
    i1 = jnp.min(jnp.where(probs == p1, lane, V7X_LANES), axis=-1, keepdims=True)
    rest = jnp.where(lane == i1, -1.0, probs)
    p2 = jnp.max(rest, axis=-1, keepdims=True)
    i2 = jnp.min(jnp.where(rest == p2, lane, V7X_LANES), axis=-1, keepdims=True)
    denom = p1 + p2
    slot = lax.broadcasted_iota(jnp.int32, (tm, TOP_K), 1)
    idx_ref[...] = jnp.where(slot == 0, i1, i2)
    gate_ref[...] = jnp.where(slot == 0, p1 / denom, p2 / denom)


def _oproj_route(x2, o2, w_o, gain, router):
    t, d = x2.shape
    tm = min(ROW_TILE, t)
    wr = jnp.pad(router, ((0, 0), (0, V7X_LANES - N_EXPERTS))).astype(BF16)
    return pl.pallas_call(
        _oproj_route_kernel,
        grid=(t // tm,),
        in_specs=[
            pl.BlockSpec((tm, d), lambda i: (i, 0)),
            pl.BlockSpec((tm, o2.shape[1]), lambda i: (i, 0)),
            _resident(w_o.shape),
            _resident((1, d)),
            _resident(wr.shape),
        ],
        out_specs=[
            pl.BlockSpec((tm, d), lambda i: (i, 0)),
            pl.BlockSpec((tm, d), lambda i: (i, 0)),
            pl.BlockSpec((tm, TOP_K), lambda i: (i, 0)),
            pl.BlockSpec((tm, TOP_K), lambda i: (i, 0)),
        ],
        out_shape=[
            jax.ShapeDtypeStruct((t, d), F32),
            jax.ShapeDtypeStruct((t, d), F32),
            jax.ShapeDtypeStruct((t, TOP_K), jnp.int32),
            jax.ShapeDtypeStruct((t, TOP_K), F32),
        ],
        compiler_params=pltpu.CompilerParams(
            dimension_semantics=("arbitrary",), vmem_limit_bytes=BIG_VMEM_LIMIT),
        name="oproj_route",
    )(x2, o2, w_o, gain.reshape(1, d), wr)


def _rank_kernel(idx_ref, rank_ref, count_ref, carry_ref):
    tm = idx_ref.shape[0]

    @pl.when(pl.program_id(0) == 0)
    def _():
        carry_ref[...] = jnp.zeros_like(carry_ref)

    idx = idx_ref[...]
    i1 = idx[:, 0:1]
    i2 = idx[:, 1:2]
    lane = lax.broadcasted_iota(jnp.int32, (tm, V7X_LANES), 1)
    onehot = jnp.where((lane == i1) | (lane == i2), 1.0, 0.0)
    r = lax.broadcasted_iota(jnp.int32, (tm, tm), 0)
    c = lax.broadcasted_iota(jnp.int32, (tm, tm), 1)
    lower = jnp.where(c < r, 1.0, 0.0).astype(BF16)
    before = _dot(lower, onehot.astype(BF16)) + carry_ref[...]
    r1 = jnp.sum(jnp.where(lane == i1, before, 0.0), axis=-1, keepdims=True)
    r2 = jnp.sum(jnp.where(lane == i2, before, 0.0), axis=-1, keepdims=True)
    slot = lax.broadcasted_iota(jnp.int32, (tm, TOP_K), 1)
    rank_ref[...] = jnp.where(slot == 0, r1, r2).astype(jnp.int32)
    total = carry_ref[...] + jnp.sum(onehot, axis=0, keepdims=True)
    carry_ref[...] = total
    count_ref[...] = total.astype(jnp.int32)


def _route_ranks(idx):
    t = idx.shape[0]
    tm = min(ROW_TILE, t)
    return pl.pallas_call(
        _rank_kernel,
        grid=(t // tm,),
        in_specs=[pl.BlockSpec((tm, TOP_K), lambda i: (i, 0))],
        out_specs=[pl.BlockSpec((tm, TOP_K), lambda i: (i, 0)),
                   pl.BlockSpec((1, V7X_LANES), lambda i: (0, 0))],
        out_shape=[jax.ShapeDtypeStruct((t, TOP_K), jnp.int32),
                   jax.ShapeDtypeStruct((1, V7X_LANES), jnp.int32)],
        scratch_shapes=[pltpu.VMEM((1, V7X_LANES), F32)],
        compiler_params=pltpu.CompilerParams(dimension_semantics=("arbitrary",)),
        name="route_ranks",
    )(idx)


def _dispatch_copy(h_ref, xs_ref, pos_ref, sem, j):
    return pltpu.make_async_copy(
        h_ref.at[pl.ds(j // TOP_K, 1), :], xs_ref.at[pl.ds(pos_ref[j], 1), :], sem)


def _dispatch_kernel(pos_ref, h_ref, xs_in_ref, xs_ref, sem):
    del xs_in_ref
    n = pos_ref.shape[0]

    def start(j, carry):
        _dispatch_copy(h_ref, xs_ref, pos_ref, sem, j).start()
        return carry

    def wait(j, carry):
        _dispatch_copy(h_ref, xs_ref, pos_ref, sem, j).wait()
        return carry

    lax.fori_loop(0, n, start, 0, unroll=8)
    lax.fori_loop(0, n, wait, 0, unroll=8)


def _dispatch(h2, pos_flat, n_rows):
    t, d = h2.shape
    tm = min(MOVE_TILE, t)
    xs0 = jnp.zeros((n_rows, d), h2.dtype)
    return pl.pallas_call(
        _dispatch_kernel,
        grid=(t // tm,),
        in_specs=[
            pl.BlockSpec((TOP_K * tm,), lambda i: (i,), memory_space=pltpu.SMEM),
            pl.BlockSpec((tm, d), lambda i: (i, 0)),
            pl.BlockSpec(memory_space=pl.ANY),
        ],
        out_specs=pl.BlockSpec(memory_space=pl.ANY),
        out_shape=jax.ShapeDtypeStruct((n_rows, d), h2.dtype),
        scratch_shapes=[pltpu.SemaphoreType.DMA(())],
        input_output_aliases={2: 0},
        compiler_params=pltpu.CompilerParams(dimension_semantics=("arbitrary",)),
        name="moe_dispatch",
    )(pos_flat, h2, xs0)


def _experts_kernel(te_ref, na_ref, x_ref, wg_ref, wu_ref, wd_ref, y_ref):
    del te_ref

    @pl.when(pl.program_id(0) < na_ref[0])
    def _():
        h = x_ref[...].astype(BF16)
        gate = _dot(h, wg_ref[0])
        up = _dot(h, wu_ref[0])
        act = (gate * jax.nn.sigmoid(gate) * up).astype(BF16)
        y_ref[...] = _dot(act, wd_ref[0])

    @pl.when(pl.program_id(0) >= na_ref[0])
    def _():
        y_ref[...] = jnp.zeros_like(y_ref)


def _experts(xs, tile_expert, n_active, wg, wu, wd):
    n_rows, d = xs.shape
    tm = GROUP_TILE
    f = wg.shape[-1]

    def row_map(i, te, na):
        return (jnp.minimum(i, na[0] - 1), 0)

    def w_map(i, te, na):
        return (te[jnp.minimum(i, na[0] - 1)], 0, 0)

    return pl.pallas_call(
        _experts_kernel,
        grid_spec=pltpu.PrefetchScalarGridSpec(
            num_scalar_prefetch=2,
            grid=(n_rows // tm,),
            in_specs=[
                pl.BlockSpec((tm, d), row_map),
                pl.BlockSpec((1, d, f), w_map),
                pl.BlockSpec((1, d, f), w_map),
                pl.BlockSpec((1, f, d), w_map),
            ],
            out_specs=pl.BlockSpec((tm, d), lambda i, te, na: (i, 0)),
        ),
        out_shape=jax.ShapeDtypeStruct((n_rows, d), F32),
        compiler_params=pltpu.CompilerParams(
            dimension_semantics=("arbitrary",), vmem_limit_bytes=BIG_VMEM_LIMIT),
        name="moe_experts",
    )(tile_expert, n_active, xs, wg, wu, wd)


def _combine_copy(ys_ref, buf_ref, pos_ref, sem, j):
    return pltpu.make_async_copy(
        ys_ref.at[pl.ds(pos_ref[j], 1), :], buf_ref.at[j % TOP_K, pl.ds(j // TOP_K, 1), :], sem)


def _combine_kernel(pos_ref, x_ref, gate_ref, ys_ref, o_ref, buf_ref, sem):
    n = pos_ref.shape[0]

    def start(j, carry):
        _combine_copy(ys_ref, buf_ref, pos_ref, sem, j).start()
        return carry

    def wait(j, carry):
        _combine_copy(ys_ref, buf_ref, pos_ref, sem, j).wait()
        return carry

    lax.fori_loop(0, n, start, 0, unroll=8)
    lax.fori_loop(0, n, wait, 0, unroll=8)
    gate = gate_ref[...]
    o_ref[...] = x_ref[...] + (gate[:, 0:1] * buf_ref[0] + gate[:, 1:2] * buf_ref[1])


def _combine(x2, gate, ys, pos_flat):
    t, d = x2.shape
    tm = min(MOVE_TILE, t)
    return pl.pallas_call(
        _combine_kernel,
        grid=(t // tm,),
        in_specs=[
            pl.BlockSpec((TOP_K * tm,), lambda i: (i,), memory_space=pltpu.SMEM),
            pl.BlockSpec((tm, d), lambda i: (i, 0)),
            pl.BlockSpec((tm, TOP_K), lambda i: (i, 0)),
            pl.BlockSpec(memory_space=pl.ANY),
        ],
        out_specs=pl.BlockSpec((tm, d), lambda i: (i, 0)),
        out_shape=jax.ShapeDtypeStruct((t, d), F32),
        scratch_shapes=[pltpu.VMEM((TOP_K, tm, d), F32), pltpu.SemaphoreType.DMA(())],
        compiler_params=pltpu.CompilerParams(dimension_semantics=("arbitrary",)),
        name="moe_combine",
    )(pos_flat, x2, gate, ys)


def _moe(x2, h2, idx, gate, wg, wu, wd):
    t, d = x2.shape
    tm = GROUP_TILE
    rank, counts = _route_ranks(idx)
    counts = counts[0, :N_EXPERTS]
    padded = (counts + tm - 1) // tm * tm
    ends = jnp.cumsum(padded)
    starts = ends - padded
    pos = starts[idx] + rank
    n_tiles = TOP_K * t // tm + N_EXPERTS
    tile_start = jnp.arange(n_tiles, dtype=jnp.int32) * tm
    tile_expert = jnp.minimum(
        jnp.sum((tile_start[:, None] >= ends[None, :]).astype(jnp.int32), axis=1), N_EXPERTS - 1)
    n_active = (ends[-1] // tm).astype(jnp.int32).reshape(1)
    pos_flat = pos.reshape(-1).astype(jnp.int32)
    xs = _dispatch(h2, pos_flat, n_tiles * tm)
    ys = _experts(xs, tile_expert.astype(jnp.int32), n_active, wg, wu, wd)
    return _combine(x2, gate, ys, pos_flat)


def kernel(x, positions, norm_mix, norm_ffn, conv_w_in, conv_w, conv_w_out, mla_w_down, mla_q_a_norm, mla_kv_a_norm, mla_w_uq, mla_w_ukv, mla_q_norm, mla_k_norm, mla_w_o, ffn_w_gate, ffn_w_up, ffn_w_down, moe_router, moe_w_gate, moe_w_up, moe_w_down):
    bsz, seq, d = x.shape
    depth = norm_mix.shape[0]
    cos_t, sinm_t = _rope_tables(positions)
    for i in range(depth):
        j = i // 2
        if i % 2 == 0:
            x = _conv_mixer(x, norm_mix[i], conv_w_in[j].astype(BF16), conv_w[j],
                            conv_w_out[j].astype(BF16))
            x = _dense_ffn(x.reshape(bsz * seq, d), norm_ffn[i], ffn_w_gate[j].astype(BF16),
                           ffn_w_up[j].astype(BF16), ffn_w_down[j].astype(BF16)).reshape(bsz, seq, d)
        else:
            q, k, v = _mla_proj(x, norm_mix[i], mla_w_down[j], mla_q_a_norm[j], mla_kv_a_norm[j],
                                mla_w_uq[j], mla_w_ukv[j], mla_q_norm[j], mla_k_norm[j], cos_t, sinm_t)
            o = _attention(q, k, v)
            x2, h2, idx, gate = _oproj_route(
                x.reshape(bsz * seq, d), o.reshape(bsz * seq, -1), mla_w_o[j].astype(BF16),
                norm_ffn[i], moe_router[j])
            x = _moe(x2, h2, idx, gate, moe_w_gate[j].astype(BF16), moe_w_up[j].astype(BF16),
                     moe_w_down[j].astype(BF16)).reshape(bsz, seq, d)
    return x
```

```python
import functools

import jax
import jax.numpy as jnp
from jax import lax
from jax.experimental import pallas as pl
from jax.experimental.pallas import tpu as pltpu

N_HEADS = 8
QK_NOPE_DIM = 128
QK_ROPE_DIM = 64
QK_HEAD_DIM = QK_NOPE_DIM + QK_ROPE_DIM
V_HEAD_DIM = 128
Q_LORA_RANK = 384
KV_LORA_RANK = 256
ROPE_THETA = 10000.0
N_EXPERTS = 8
TOP_K = 2
EPS = 1e-6

V7X_LANES = 128
V7X_SUBLANES = 8
V7X_VMEM_BYTES = 64 * 1024 * 1024
BIG_VMEM_LIMIT = V7X_VMEM_BYTES - 8 * 1024 * 1024

QK_PAD_DIM = 2 * V7X_LANES
MASK_VALUE = -1e30

ROW_TILE = 512
SUB_TILE = 256
ATTN_TILE = 256
GROUP_TILE = 512
MOVE_TILE = 256

F32 = jnp.float32
BF16 = jnp.bfloat16


def _rms(x, gain):
    return x * lax.rsqrt(jnp.mean(x * x, axis=-1, keepdims=True) + EPS) * gain


def _dot(a, b):
    return jnp.dot(a, b, preferred_element_type=F32)


def _resident(shape):
    nd = len(shape)
    return pl.BlockSpec(shape, lambda *_: (0,) * nd, pipeline_mode=pl.Buffered(1))


def _conv_kernel(x_ref, g_ref, w_in_ref, cw_ref, w_out_ref, o_ref, carry_ref):
    d = x_ref.shape[-1]
    tm = x_ref.shape[1]

    @pl.when(pl.program_id(1) == 0)
    def _():
        carry_ref[...] = jnp.zeros_like(carry_ref)

    x = x_ref[0]
    h = _rms(x, g_ref[...]).astype(BF16)
    y = _dot(h, w_in_ref[...])
    b_gate = y[:, :d]
    v = y[:, d:2 * d] * y[:, 2 * d:]
    prev = carry_ref[...]
    p1 = prev[V7X_SUBLANES - 1:V7X_SUBLANES]
    p2 = prev[V7X_SUBLANES - 2:V7X_SUBLANES - 1]
    rows = lax.broadcasted_iota(jnp.int32, (tm, d), 0)
    v1 = jnp.where(rows == 0, p1, pltpu.roll(v, 1, 0))
    v2 = jnp.where(rows == 0, p2, jnp.where(rows == 1, p1, pltpu.roll(v, 2, 0)))
    carry_ref[...] = v[tm - V7X_SUBLANES:]
    cw = cw_ref[...]
    conv = v2 * cw[0:1] + v1 * cw[1:2] + v * cw[2:3]
    z = (b_gate * conv).astype(BF16)
    o_ref[0] = x + _dot(z, w_out_ref[...])


def _conv_mixer(x, gain, w_in, conv_w, w_out):
    bsz, seq, d = x.shape
    tm = min(ROW_TILE, seq)
    return pl.pallas_call(
        _conv_kernel,
        grid=(bsz, seq // tm),
        in_specs=[
            pl.BlockSpec((1, tm, d), lambda b, s: (b, s, 0)),
            _resident((1, d)),
            _resident(w_in.shape),
            _resident(conv_w.shape),
            _resident(w_out.shape),
        ],
        out_specs=pl.BlockSpec((1, tm, d), lambda b, s: (b, s, 0)),
        out_shape=jax.ShapeDtypeStruct(x.shape, F32),
        scratch_shapes=[pltpu.VMEM((V7X_SUBLANES, d), F32)],
        compiler_params=pltpu.CompilerParams(
            dimension_semantics=("arbitrary", "arbitrary"), vmem_limit_bytes=BIG_VMEM_LIMIT),
        name="conv_mixer",
    )(x, gain.reshape(1, d), w_in, conv_w, w_out)


def _ffn_kernel(x_ref, g_ref, wg_ref, wu_ref, wd_ref, o_ref):
    x = x_ref[...]
    h = _rms(x, g_ref[...]).astype(BF16)
    gate = _dot(h, wg_ref[...])
    up = _dot(h, wu_ref[...])
    act = (gate * jax.nn.sigmoid(gate) * up).astype(BF16)
    o_ref[...] = x + _dot(act, wd_ref[...])


def _dense_ffn(x2, gain, wg, wu, wd):
    t, d = x2.shape
    tm = min(ROW_TILE, t)
    return pl.pallas_call(
        _ffn_kernel,
        grid=(t // tm,),
        in_specs=[
            pl.BlockSpec((tm, d), lambda i: (i, 0)),
            _resident((1, d)),
            _resident(wg.shape),
            _resident(wu.shape),
            _resident(wd.shape),
        ],
        out_specs=pl.BlockSpec((tm, d), lambda i: (i, 0)),
        out_shape=jax.ShapeDtypeStruct(x2.shape, F32),
        compiler_params=pltpu.CompilerParams(
            dimension_semantics=("arbitrary",), vmem_limit_bytes=BIG_VMEM_LIMIT),
        name="dense_ffn",
    )(x2, gain.reshape(1, d), wg, wu, wd)


def _rope_table_kernel(pos_ref, inv_ref, cos_ref, sin_ref, nsin_ref):
    ang = pos_ref[...] * inv_ref[...]
    s = jnp.sin(ang)
    cos_ref[...] = jnp.cos(ang)
    sin_ref[...] = s
    nsin_ref[...] = -s


def _rope_tables(positions):
    bsz, seq = positions.shape
    half = QK_ROPE_DIM // 2
    t = bsz * seq
    inv_freq = ROPE_THETA ** (-jnp.arange(0, QK_ROPE_DIM, 2, dtype=F32) / QK_ROPE_DIM)
    per_row = V7X_LANES // half
    rows = t // per_row
    pos_rep = jnp.repeat(positions.reshape(-1).astype(F32), half).reshape(rows, V7X_LANES)
    inv_rep = jnp.tile(inv_freq, per_row).reshape(1, V7X_LANES)
    tr = min(1024, rows)
    shp = jax.ShapeDtypeStruct((rows, V7X_LANES), F32)
    cos, sin, nsin = pl.pallas_call(
        _rope_table_kernel,
        grid=(rows // tr,),
        in_specs=[pl.BlockSpec((tr, V7X_LANES), lambda i: (i, 0)),
                  pl.BlockSpec((1, V7X_LANES), lambda i: (0, 0))],
        out_specs=[pl.BlockSpec((tr, V7X_LANES), lambda i: (i, 0))] * 3,
        out_shape=[shp, shp, shp],
        name="rope_tables",
    )(pos_rep, inv_rep)
    cos, sin, nsin = (a.reshape(bsz, seq, half) for a in (cos, sin, nsin))
    zeros = jnp.zeros((bsz, seq, V7X_LANES - QK_ROPE_DIM), F32)
    cos_t = jnp.concatenate([cos, cos, zeros], axis=-1)
    sinm_t = jnp.concatenate([nsin, sin, zeros], axis=-1)
    return cos_t, sinm_t


def _mla_proj_kernel(x_ref, g_ref, wdn_ref, qa_ref, kva_ref, wuq_ref, wukv_ref,
                     qn_ref, kn_ref, cos_ref, sinm_ref, q_ref, k_ref, v_ref):
    qn = qn_ref[...]
    kn = kn_ref[...]
    scale = QK_HEAD_DIM ** -0.5
    tm = x_ref.shape[1]
    sub = min(SUB_TILE, tm)
    for r0 in range(0, tm, sub):
        rows = slice(r0, r0 + sub)
        h = _rms(x_ref[0, rows, :], g_ref[...]).astype(BF16)
        down = _dot(h, wdn_ref[...])
        c_q = _rms(down[:, :Q_LORA_RANK], qa_ref[...]).astype(BF16)
        c_kv = _rms(down[:, Q_LORA_RANK:Q_LORA_RANK + KV_LORA_RANK], kva_ref[...]).astype(BF16)
        k_rope = down[:, Q_LORA_RANK + KV_LORA_RANK:]
        q = _dot(c_q, wuq_ref[...])
        kv = _dot(c_kv, wukv_ref[...])
        cos = cos_ref[0, rows, :]
        sinm = sinm_ref[0, rows, :]

        def rope(r, cos=cos, sinm=sinm):
            return r * cos + (pltpu.roll(r, 32, 1) + pltpu.roll(r, 96, 1)) * sinm

        kr = k_rope * kn[:, V7X_LANES:]
        kr_ss = jnp.sum(k_rope * k_rope, axis=-1, keepdims=True)
        kr_rot = rope(kr)
        for hd in range(N_HEADS):
            qh = q[:, hd * QK_PAD_DIM:(hd + 1) * QK_PAD_DIM]
            inv_q = lax.rsqrt(jnp.sum(qh * qh, axis=-1, keepdims=True) / QK_HEAD_DIM + EPS) * scale
            qh = qh * inv_q * qn
            q_ref[0, hd, rows, :V7X_LANES] = qh[:, :V7X_LANES].astype(BF16)
            q_ref[0, hd, rows, V7X_LANES:] = rope(qh[:, V7X_LANES:]).astype(BF16)
            k_nope = kv[:, hd * QK_PAD_DIM:hd * QK_PAD_DIM + QK_NOPE_DIM]
            ss = jnp.sum(k_nope * k_nope, axis=-1, keepdims=True) + kr_ss
            inv_k = lax.rsqrt(ss / QK_HEAD_DIM + EPS)
            k_ref[0, hd, rows, :V7X_LANES] = (k_nope * inv_k * kn[:, :V7X_LANES]).astype(BF16)
            k_ref[0, hd, rows, V7X_LANES:] = (kr_rot * inv_k).astype(BF16)
            v_ref[0, hd, rows, :] = kv[:, hd * QK_PAD_DIM + QK_NOPE_DIM:(hd + 1) * QK_PAD_DIM].astype(BF16)


def _pad_head_gain(gain):
    return jnp.concatenate([gain, jnp.zeros((QK_PAD_DIM - QK_HEAD_DIM,), F32)]).reshape(1, QK_PAD_DIM)


def _mla_proj(x, gain, w_down, q_a_norm, kv_a_norm, w_uq, w_ukv, q_norm, k_norm, cos_t, sinm_t):
    bsz, seq, d = x.shape
    tm = min(ROW_TILE, seq)
    n_down = w_down.shape[1]
    pad_down = (-n_down) % (2 * V7X_LANES)
    wdn = jnp.pad(w_down, ((0, 0), (0, pad_down))).astype(BF16)
    wuq = w_uq.reshape(Q_LORA_RANK, N_HEADS, QK_HEAD_DIM)
    wuq = jnp.pad(wuq, ((0, 0), (0, 0), (0, QK_PAD_DIM - QK_HEAD_DIM)))
    wuq = wuq.reshape(Q_LORA_RANK, N_HEADS * QK_PAD_DIM).astype(BF16)
    wukv = w_ukv.astype(BF16)
    row_spec = pl.BlockSpec((1, tm, d), lambda b, s: (b, s, 0))
    tab_spec = pl.BlockSpec((1, tm, V7X_LANES), lambda b, s: (b, s, 0))
    qk_spec = pl.BlockSpec((1, N_HEADS, tm, QK_PAD_DIM), lambda b, s: (b, 0, s, 0))
    v_spec = pl.BlockSpec((1, N_HEADS, tm, V_HEAD_DIM), lambda b, s: (b, 0, s, 0))
    qk_shape = jax.ShapeDtypeStruct((bsz, N_HEADS, seq, QK_PAD_DIM), BF16)
    v_shape = jax.ShapeDtypeStruct((bsz, N_HEADS, seq, V_HEAD_DIM), BF16)
    return pl.pallas_call(
        _mla_proj_kernel,
        grid=(bsz, seq // tm),
        in_specs=[
            row_spec,
            _resident((1, d)),
            _resident(wdn.shape),
            _resident((1, Q_LORA_RANK)),
            _resident((1, KV_LORA_RANK)),
            _resident(wuq.shape),
            _resident(wukv.shape),
            _resident((1, QK_PAD_DIM)),
            _resident((1, QK_PAD_DIM)),
            tab_spec,
            tab_spec,
        ],
        out_specs=[qk_spec, qk_spec, v_spec],
        out_shape=[qk_shape, qk_shape, v_shape],
        compiler_params=pltpu.CompilerParams(
            dimension_semantics=("arbitrary", "arbitrary"), vmem_limit_bytes=BIG_VMEM_LIMIT),
        name="mla_proj",
    )(x, gain.reshape(1, d), wdn, q_a_norm.reshape(1, -1), kv_a_norm.reshape(1, -1), wuq, wukv,
      _pad_head_gain(q_norm), _pad_head_gain(k_norm), cos_t, sinm_t)


def _attn_kernel(q_ref, k_ref, v_ref, o_ref):
    seq = q_ref.shape[2]
    t = min(ATTN_TILE, seq)
    dims = (((1,), (1,)), ((), ()))
    row = lax.broadcasted_iota(jnp.int32, (t, t), 0)
    col = lax.broadcasted_iota(jnp.int32, (t, t), 1)
    for qi in range(seq // t):
        q = q_ref[0, 0, qi * t:(qi + 1) * t, :]
        m = l = acc = None
        for ki in range(qi + 1):
            k = k_ref[0, 0, ki * t:(ki + 1) * t, :]
            v = v_ref[0, 0, ki * t:(ki + 1) * t, :]
            s = lax.dot_general(q, k, dims, preferred_element_type=F32)
            if ki == qi:
                s = jnp.where(col <= row, s, MASK_VALUE)
            s_max = jnp.max(s, axis=-1, keepdims=True)
            if ki == 0:
                m = s_max
                p = jnp.exp(s - m)
                l = jnp.sum(p, axis=-1, keepdims=True)
                acc = _dot(p.astype(BF16), v)
            else:
                m_new = jnp.maximum(m, s_max)
                alpha = jnp.exp(m - m_new)
                p = jnp.exp(s - m_new)
                l = alpha * l + jnp.sum(p, axis=-1, keepdims=True)
                acc = alpha * acc + _dot(p.astype(BF16), v)
                m = m_new
        o_ref[0, qi * t:(qi + 1) * t, :] = (acc / l).astype(o_ref.dtype)


def _attention(q, k, v):
    bsz, nh, seq, _ = q.shape
    dv = v.shape[-1]
    head_spec = lambda last: pl.BlockSpec((1, 1, seq, last), lambda b, h: (b, h, 0, 0))
    return pl.pallas_call(
        _attn_kernel,
        grid=(bsz, nh),
        in_specs=[head_spec(q.shape[-1]), head_spec(k.shape[-1]), head_spec(dv)],
        out_specs=pl.BlockSpec((1, seq, dv), lambda b, h: (b, 0, h)),
        out_shape=jax.ShapeDtypeStruct((bsz, seq, nh * dv), BF16),
        compiler_params=pltpu.CompilerParams(dimension_semantics=("arbitrary", "arbitrary")),
        name="causal_attention",
    )(q, k, v)


def _oproj_route_kernel(x_ref, o_ref, wo_ref, g_ref, wr_ref, xo_ref, h_ref, idx_ref, gate_ref):
    tm = x_ref.shape[0]
    x = x_ref[...] + _dot(o_ref[...], wo_ref[...])
    xo_ref[...] = x
    hn = _rms(x, g_ref[...])
    h_ref[...] = hn
    logits = _dot(hn.astype(BF16), wr_ref[...])
    lane = lax.broadcasted_iota(jnp.int32, logits.shape, 1)
    logits = jnp.where(lane < N_EXPERTS, logits, MASK_VALUE)
    e = jnp.exp(logits - jnp.max(logits, axis=-1, keepdims=True))
    probs = e / jnp.sum(e, axis=-1, keepdims=True)
    probs = jnp.where(lane < N_EXPERTS, probs, -1.0)
    p1 = jnp.max(probs, axis=-1, keepdims=True)
    i1 = jnp.min(jnp.where(probs == p1, lane, V7X_LANES), axis=-1, keepdims=True)
    rest = jnp.where(lane == i1, -1.0, probs)
    p2 = jnp.max(rest, axis=-1, keepdims=True)
    i2 = jnp.min(jnp.where(rest == p2, lane, V7X_LANES), axis=-1, keepdims=True)
    denom = p1 + p2
    slot = lax.broadcasted_iota(jnp.int32, (tm, TOP_K), 1)
    idx_ref[...] = jnp.where(slot == 0, i1, i2)
    gate_ref[...] = jnp.where(slot == 0, p1 / denom, p2 / denom)


def _oproj_route(x2, o2, w_o, gain, router):
    t, d = x2.shape
    tm = min(ROW_TILE, t)
    wr = jnp.pad(router, ((0, 0), (0, V7X_LANES - N_EXPERTS))).astype(BF16)
    return pl.pallas_call(
        _oproj_route_kernel,
        grid=(t // tm,),
        in_specs=[
            pl.BlockSpec((tm, d), lambda i: (i, 0)),
            pl.BlockSpec((tm, o2.shape[1]), lambda i: (i, 0)),
            _resident(w_o.shape),
            _resident((1, d)),
            _resident(wr.shape),
        ],
        out_specs=[
            pl.BlockSpec((tm, d), lambda i: (i, 0)),
            pl.BlockSpec((tm, d), lambda i: (i, 0)),
            pl.BlockSpec((tm, TOP_K), lambda i: (i, 0)),
            pl.BlockSpec((tm, TOP_K), lambda i: (i, 0)),
        ],
        out_shape=[
            jax.ShapeDtypeStruct((t, d), F32),
            jax.ShapeDtypeStruct((t, d), F32),
            jax.ShapeDtypeStruct((t, TOP_K), jnp.int32),
            jax.ShapeDtypeStruct((t, TOP_K), F32),
        ],
        compiler_params=pltpu.CompilerParams(
            dimension_semantics=("arbitrary",), vmem_limit_bytes=BIG_VMEM_LIMIT),
        name="oproj_route",
    )(x2, o2, w_o, gain.reshape(1, d), wr)


def _rank_kernel(idx_ref, rank_ref, count_ref, carry_ref):
    tm = idx_ref.shape[0]

    @pl.when(pl.program_id(0) == 0)
    def _():
        carry_ref[...] = jnp.zeros_like(carry_ref)

    idx = idx_ref[...]
    i1 = idx[:, 0:1]
    i2 = idx[:, 1:2]
    lane = lax.broadcasted_iota(jnp.int32, (tm, V7X_LANES), 1)
    onehot = jnp.where((lane == i1) | (lane == i2), 1.0, 0.0)
    r = lax.broadcasted_iota(jnp.int32, (tm, tm), 0)
    c = lax.broadcasted_iota(jnp.int32, (tm, tm), 1)
    lower = jnp.where(c < r, 1.0, 0.0).astype(BF16)
    before = _dot(lower, onehot.astype(BF16)) + carry_ref[...]
    r1 = jnp.sum(jnp.where(lane == i1, before, 0.0), axis=-1, keepdims=True)
    r2 = jnp.sum(jnp.where(lane == i2, before, 0.0), axis=-1, keepdims=True)
    slot = lax.broadcasted_iota(jnp.int32, (tm, TOP_K), 1)
    rank_ref[...] = jnp.where(slot == 0, r1, r2).astype(jnp.int32)
    total = carry_ref[...] + jnp.sum(onehot, axis=0, keepdims=True)
    carry_ref[...] = total
    count_ref[...] = total.astype(jnp.int32)


def _route_ranks(idx):
    t = idx.shape[0]
    tm = min(ROW_TILE, t)
    return pl.pallas_call(
        _rank_kernel,
        grid=(t // tm,),
        in_specs=[pl.BlockSpec((tm, TOP_K), lambda i: (i, 0))],
        out_specs=[pl.BlockSpec((tm, TOP_K), lambda i: (i, 0)),
                   pl.BlockSpec((1, V7X_LANES), lambda i: (0, 0))],
        out_shape=[jax.ShapeDtypeStruct((t, TOP_K), jnp.int32),
                   jax.ShapeDtypeStruct((1, V7X_LANES), jnp.int32)],
        scratch_shapes=[pltpu.VMEM((1, V7X_LANES), F32)],
        compiler_params=pltpu.CompilerParams(dimension_semantics=("arbitrary",)),
        name="route_ranks",
    )(idx)


def _dispatch_copies(h_ref, xs_ref, pos_ref, sem, group):
    base = pl.multiple_of(group * V7X_SUBLANES, V7X_SUBLANES)
    copies = []
    for u in range(V7X_SUBLANES):
        for k in range(TOP_K):
            dst_row = pos_ref[TOP_K * base + (TOP_K * u + k)]
            copies.append(pltpu.make_async_copy(
                h_ref.at[pl.ds(base + u, 1), :], xs_ref.at[pl.ds(dst_row, 1), :], sem))
    return copies


def _dispatch_kernel(pos_ref, h_ref, xs_in_ref, xs_ref, sem):
    del xs_in_ref
    groups = h_ref.shape[0] // V7X_SUBLANES

    def start(g, carry):
        for i, cp in enumerate(_dispatch_copies(h_ref, xs_ref, pos_ref, sem, g)):
            cp.start(priority=i % 2)
        return carry

    def wait(g, carry):
        for cp in _dispatch_copies(h_ref, xs_ref, pos_ref, sem, g):
            cp.wait()
        return carry

    lax.fori_loop(0, groups, start, 0)
    lax.fori_loop(0, groups, wait, 0)


def _dispatch(h2, pos_flat, n_rows):
    t, d = h2.shape
    tm = min(MOVE_TILE, t)
    xs0 = jnp.zeros((n_rows, d), h2.dtype)
    return pl.pallas_call(
        _dispatch_kernel,
        grid=(t // tm,),
        in_specs=[
            pl.BlockSpec((TOP_K * tm,), lambda i: (i,), memory_space=pltpu.SMEM),
            pl.BlockSpec((tm, d), lambda i: (i, 0)),
            pl.BlockSpec(memory_space=pl.ANY),
        ],
        out_specs=pl.BlockSpec(memory_space=pl.ANY),
        out_shape=jax.ShapeDtypeStruct((n_rows, d), h2.dtype),
        scratch_shapes=[pltpu.SemaphoreType.DMA(())],
        input_output_aliases={2: 0},
        compiler_params=pltpu.CompilerParams(dimension_semantics=("arbitrary",)),
        name="moe_dispatch",
    )(pos_flat, h2, xs0)


def _experts_kernel(te_ref, na_ref, x_ref, wg_ref, wu_ref, wd_ref, y_ref):
    del te_ref

    @pl.when(pl.program_id(0) < na_ref[0])
    def _():
        h = x_ref[...].astype(BF16)
        gate = _dot(h, wg_ref[0])
        up = _dot(h, wu_ref[0])
        act = (gate * jax.nn.sigmoid(gate) * up).astype(BF16)
        y_ref[...] = _dot(act, wd_ref[0])

    @pl.when(pl.program_id(0) >= na_ref[0])
    def _():
        y_ref[...] = jnp.zeros_like(y_ref)


def _experts(xs, tile_expert, n_active, wg, wu, wd):
    n_rows, d = xs.shape
    tm = GROUP_TILE
    f = wg.shape[-1]

    def row_map(i, te, na):
        return (jnp.minimum(i, na[0] - 1), 0)

    def w_map(i, te, na):
        return (te[jnp.minimum(i, na[0] - 1)], 0, 0)

    return pl.pallas_call(
        _experts_kernel,
        grid_spec=pltpu.PrefetchScalarGridSpec(
            num_scalar_prefetch=2,
            grid=(n_rows // tm,),
            in_specs=[
                pl.BlockSpec((tm, d), row_map),
                pl.BlockSpec((1, d, f), w_map),
                pl.BlockSpec((1, d, f), w_map),
                pl.BlockSpec((1, f, d), w_map),
            ],
            out_specs=pl.BlockSpec((tm, d), lambda i, te, na: (i, 0)),
        ),
        out_shape=jax.ShapeDtypeStruct((n_rows, d), F32),
        compiler_params=pltpu.CompilerParams(
            dimension_semantics=("arbitrary",), vmem_limit_bytes=BIG_VMEM_LIMIT),
        name="moe_experts",
    )(tile_expert, n_active, xs, wg, wu, wd)


def _combine_copies(ys_ref, buf_ref, pos_ref, sem, group):
    base = pl.multiple_of(group * V7X_SUBLANES, V7X_SUBLANES)
    copies = []
    for u in range(V7X_SUBLANES):
        for k in range(TOP_K):
            src_row = pos_ref[TOP_K * base + (TOP_K * u + k)]
            copies.append(pltpu.make_async_copy(
                ys_ref.at[pl.ds(src_row, 1), :], buf_ref.at[k, pl.ds(base + u, 1), :], sem))
    return copies


def _combine_kernel(pos_ref, x_ref, gate_ref, ys_ref, o_ref, buf_ref, sem):
    groups = x_ref.shape[0] // V7X_SUBLANES

    def start(g, carry):
        for i, cp in enumerate(_combine_copies(ys_ref, buf_ref, pos_ref, sem, g)):
            cp.start(priority=i % 2)
        return carry

    def wait(g, carry):
        for cp in _combine_copies(ys_ref, buf_ref, pos_ref, sem, g):
            cp.wait()
        return carry

    lax.fori_loop(0, groups, start, 0)
    lax.fori_loop(0, groups, wait, 0)
    gate = gate_ref[...]
    o_ref[...] = x_ref[...] + (gate[:, 0:1] * buf_ref[0] + gate[:, 1:2] * buf_ref[1])


def _combine(x2, gate, ys, pos_flat):
    t, d = x2.shape
    tm = min(MOVE_TILE, t)
    return pl.pallas_call(
        _combine_kernel,
        grid=(t // tm,),
        in_specs=[
            pl.BlockSpec((TOP_K * tm,), lambda i: (i,), memory_space=pltpu.SMEM),
            pl.BlockSpec((tm, d), lambda i: (i, 0)),
            pl.BlockSpec((tm, TOP_K), lambda i: (i, 0)),
            pl.BlockSpec(memory_space=pl.ANY),
        ],
        out_specs=pl.BlockSpec((tm, d), lambda i: (i, 0)),
        out_shape=jax.ShapeDtypeStruct((t, d), F32),
        scratch_shapes=[pltpu.VMEM((TOP_K, tm, d), F32), pltpu.SemaphoreType.DMA(())],
        compiler_params=pltpu.CompilerParams(dimension_semantics=("arbitrary",)),
        name="moe_combine",
    )(pos_flat, x2, gate, ys)


def _moe(x2, h2, idx, gate, wg, wu, wd):
    t, d = x2.shape
    tm = GROUP_TILE
    rank, counts = _route_ranks(idx)
    counts = counts[0, :N_EXPERTS]
    padded = (counts + tm - 1) // tm * tm
    ends = jnp.cumsum(padded)
    starts = ends - padded
    pos = starts[idx] + rank
    n_tiles = TOP_K * t // tm + N_EXPERTS
    tile_start = jnp.arange(n_tiles, dtype=jnp.int32) * tm
    tile_expert = jnp.minimum(
        jnp.sum((tile_start[:, None] >= ends[None, :]).astype(jnp.int32), axis=1), N_EXPERTS - 1)
    n_active = (ends[-1] // tm).astype(jnp.int32).reshape(1)
    pos_flat = pos.reshape(-1).astype(jnp.int32)
    xs = _dispatch(h2, pos_flat, n_tiles * tm)
    ys = _experts(xs, tile_expert.astype(jnp.int32), n_active, wg, wu, wd)
    return _combine(x2, gate, ys, pos_flat)


def kernel(x, positions, norm_mix, norm_ffn, conv_w_in, conv_w, conv_w_out, mla_w_down, mla_q_a_norm, mla_kv_a_norm, mla_w_uq, mla_w_ukv, mla_q_norm, mla_k_norm, mla_w_o, ffn_w_gate, ffn_w_up, ffn_w_down, moe_router, moe_w_gate, moe_w_up, moe_w_down):
    bsz, seq, d = x.shape
    depth = norm_mix.shape[0]
    cos_t, sinm_t = _rope_tables(positions)
    for i in range(depth):
        j = i // 2
        if i % 2 == 0:
            x = _conv_mixer(x, norm_mix[i], conv_w_in[j].astype(BF16), conv_w[j],
                            conv_w_out[j].astype(BF16))
            x = _dense_ffn(x.reshape(bsz * seq, d), norm_ffn[i], ffn_w_gate[j].astype(BF16),
                           ffn_w_up[j].astype(BF16), ffn_w_down[j].astype(BF16)).reshape(bsz, seq, d)
        else:
            q, k, v = _mla_proj(x, norm_mix[i], mla_w_down[j], mla_q_a_norm[j], mla_kv_a_norm[j],
                                mla_w_uq[j], mla_w_ukv[j], mla_q_norm[j], mla_k_norm[j], cos_t, sinm_t)
            o = _attention(q, k, v)
            x2, h2, idx, gate = _oproj_route(
                x.reshape(bsz * seq, d), o.reshape(bsz * seq, -1), mla_w_o[j].astype(BF16),
                norm_ffn[i], moe_router[j])
            x = _moe(x2, h2, idx, gate, moe_w_gate[j].astype(BF16), moe_w_up[j].astype(BF16),
                     moe_w_down[j].astype(BF16)).reshape(bsz, seq, d)
    return x
```

```python
import jax
import jax.numpy as jnp
from jax import lax
from jax.experimental import pallas as pl
from jax.experimental.pallas import tpu as pltpu

N_HEADS = 8
QK_NOPE_DIM = 128
QK_ROPE_DIM = 64
QK_HEAD_DIM = QK_NOPE_DIM + QK_ROPE_DIM
V_HEAD_DIM = 128
Q_LORA_RANK = 384
KV_LORA_RANK = 256
ROPE_THETA = 10000.0
N_EXPERTS = 8
TOP_K = 2
EPS = 1e-6

V7X_LANES = 128
V7X_SUBLANES = 8
V7X_VMEM_BYTES = 64 * 1024 * 1024
BIG_VMEM_LIMIT = V7X_VMEM_BYTES - 8 * 1024 * 1024

QK_PAD_DIM = 2 * V7X_LANES
MASK_VALUE = -1e30

ROW_TILE = 512
SUB_TILE = 256
ATTN_TILE = 256
GROUP_TILE = 512
MOVE_TILE = 256

F32 = jnp.float32
BF16 = jnp.bfloat16


def _rms(x, gain):
    return x * lax.rsqrt(jnp.mean(x * x, axis=-1, keepdims=True) + EPS) * gain


def _dot(a, b):
    return jnp.dot(a, b, preferred_element_type=F32)


def _resident(shape, layer=None):
    nd = len(shape)
    if layer is None:
        return pl.BlockSpec(shape, lambda *_: (0,) * nd, pipeline_mode=pl.Buffered(1))
    return pl.BlockSpec((None,) + tuple(shape[1:]), lambda *_: (layer,) + (0,) * (nd - 1),
                        pipeline_mode=pl.Buffered(1))


def _conv_kernel(x_ref, g_ref, w_in_ref, cw_ref, w_out_ref, o_ref, carry_ref):
    d = x_ref.shape[-1]
    tm = x_ref.shape[1]

    @pl.when(pl.program_id(1) == 0)
    def _():
        carry_ref[...] = jnp.zeros_like(carry_ref)

    x = x_ref[0]
    h = _rms(x, g_ref[...]).astype(BF16)
    y = _dot(h, w_in_ref[...])
    b_gate = y[:, :d]
    v = y[:, d:2 * d] * y[:, 2 * d:]
    prev = carry_ref[...]
    p1 = prev[V7X_SUBLANES - 1:V7X_SUBLANES]
    p2 = prev[V7X_SUBLANES - 2:V7X_SUBLANES - 1]
    rows = lax.broadcasted_iota(jnp.int32, (tm, d), 0)
    v1 = jnp.where(rows == 0, p1, pltpu.roll(v, 1, 0))
    v2 = jnp.where(rows == 0, p2, jnp.where(rows == 1, p1, pltpu.roll(v, 2, 0)))
    carry_ref[...] = v[tm - V7X_SUBLANES:]
    cw = cw_ref[...]
    conv = v2 * cw[0:1] + v1 * cw[1:2] + v * cw[2:3]
    z = (b_gate * conv).astype(BF16)
    o_ref[0] = x + _dot(z, w_out_ref[...])


def _conv_mixer(x, gain, w_in, conv_w, w_out, layer):
    bsz, seq, d = x.shape
    tm = min(ROW_TILE, seq)
    return pl.pallas_call(
        _conv_kernel,
        grid=(bsz, seq // tm),
        in_specs=[
            pl.BlockSpec((1, tm, d), lambda b, s: (b, s, 0)),
            _resident((1, d)),
            _resident(w_in.shape, layer),
            _resident(conv_w.shape, layer),
            _resident(w_out.shape, layer),
        ],
        out_specs=pl.BlockSpec((1, tm, d), lambda b, s: (b, s, 0)),
        out_shape=jax.ShapeDtypeStruct(x.shape, F32),
        scratch_shapes=[pltpu.VMEM((V7X_SUBLANES, d), F32)],
        compiler_params=pltpu.CompilerParams(
            dimension_semantics=("arbitrary", "arbitrary"), vmem_limit_bytes=BIG_VMEM_LIMIT),
        name="conv_mixer",
    )(x, gain.reshape(1, d), w_in, conv_w, w_out)


def _ffn_kernel(x_ref, g_ref, wg_ref, wu_ref, wd_ref, o_ref):
    x = x_ref[...]
    h = _rms(x, g_ref[...]).astype(BF16)
    gate = _dot(h, wg_ref[...])
    up = _dot(h, wu_ref[...])
    act = (gate * jax.nn.sigmoid(gate) * up).astype(BF16)
    o_ref[...] = x + _dot(act, wd_ref[...])


def _dense_ffn(x2, gain, wg, wu, wd, layer):
    t, d = x2.shape
    tm = min(ROW_TILE, t)
    return pl.pallas_call(
        _ffn_kernel,
        grid=(t // tm,),
        in_specs=[
            pl.BlockSpec((tm, d), lambda i: (i, 0)),
            _resident((1, d)),
            _resident(wg.shape, layer),
            _resident(wu.shape, layer),
            _resident(wd.shape, layer),
        ],
        out_specs=pl.BlockSpec((tm, d), lambda i: (i, 0)),
        out_shape=jax.ShapeDtypeStruct(x2.shape, F32),
        compiler_params=pltpu.CompilerParams(
            dimension_semantics=("arbitrary",), vmem_limit_bytes=BIG_VMEM_LIMIT),
        name="dense_ffn",
    )(x2, gain.reshape(1, d), wg, wu, wd)


def _rope_table_kernel(pos_ref, inv_ref, cos_ref, sin_ref, nsin_ref):
    ang = pos_ref[...] * inv_ref[...]
    s = jnp.sin(ang)
    cos_ref[...] = jnp.cos(ang)
    sin_ref[...] = s
    nsin_ref[...] = -s


def _rope_tables(positions):
    bsz, seq = positions.shape
    half = QK_ROPE_DIM // 2
    t = bsz * seq
    inv_freq = ROPE_THETA ** (-jnp.arange(0, QK_ROPE_DIM, 2, dtype=F32) / QK_ROPE_DIM)
    per_row = V7X_LANES // half
    rows = t // per_row
    pos_rep = jnp.repeat(positions.reshape(-1).astype(F32), half).reshape(rows, V7X_LANES)
    inv_rep = jnp.tile(inv_freq, per_row).reshape(1, V7X_LANES)
    tr = min(1024, rows)
    shp = jax.ShapeDtypeStruct((rows, V7X_LANES), F32)
    cos, sin, nsin = pl.pallas_call(
        _rope_table_kernel,
        grid=(rows // tr,),
        in_specs=[pl.BlockSpec((tr, V7X_LANES), lambda i: (i, 0)),
                  pl.BlockSpec((1, V7X_LANES), lambda i: (0, 0))],
        out_specs=[pl.BlockSpec((tr, V7X_LANES), lambda i: (i, 0))] * 3,
        out_shape=[shp, shp, shp],
        name="rope_tables",
    )(pos_rep, inv_rep)
    cos, sin, nsin = (a.reshape(bsz, seq, half) for a in (cos, sin, nsin))
    zeros = jnp.zeros((bsz, seq, V7X_LANES - QK_ROPE_DIM), F32)
    cos_t = jnp.concatenate([cos, cos, zeros], axis=-1)
    sinm_t = jnp.concatenate([nsin, sin, zeros], axis=-1)
    return cos_t, sinm_t


def _mla_proj_kernel(x_ref, g_ref, wdn_ref, qa_ref, kva_ref, wuq_ref, wukv_ref,
                     qn_ref, kn_ref, cos_ref, sinm_ref, q_ref, k_ref, v_ref):
    qn = qn_ref[...]
    kn = kn_ref[...]
    scale = QK_HEAD_DIM ** -0.5
    tm = x_ref.shape[1]
    sub = min(SUB_TILE, tm)
    for r0 in range(0, tm, sub):
        rows = slice(r0, r0 + sub)
        h = _rms(x_ref[0, rows, :], g_ref[...]).astype(BF16)
        down = _dot(h, wdn_ref[...])
        c_q = _rms(down[:, :Q_LORA_RANK], qa_ref[...]).astype(BF16)
        c_kv = _rms(down[:, Q_LORA_RANK:Q_LORA_RANK + KV_LORA_RANK], kva_ref[...]).astype(BF16)
        k_rope = down[:, Q_LORA_RANK + KV_LORA_RANK:]
        q = _dot(c_q, wuq_ref[...])
        kv = _dot(c_kv, wukv_ref[...])
        cos = cos_ref[0, rows, :]
        sinm = sinm_ref[0, rows, :]

        def rope(r, cos=cos, sinm=sinm):
            return r * cos + (pltpu.roll(r, 32, 1) + pltpu.roll(r, 96, 1)) * sinm

        kr = k_rope * kn[:, V7X_LANES:]
        kr_ss = jnp.sum(k_rope * k_rope, axis=-1, keepdims=True)
        kr_rot = rope(kr)
        for hd in range(N_HEADS):
            qh = q[:, hd * QK_PAD_DIM:(hd + 1) * QK_PAD_DIM]
            inv_q = lax.rsqrt(jnp.sum(qh * qh, axis=-1, keepdims=True) / QK_HEAD_DIM + EPS) * scale
            qh = qh * inv_q * qn
            q_ref[0, hd, rows, :V7X_LANES] = qh[:, :V7X_LANES].astype(BF16)
            q_ref[0, hd, rows, V7X_LANES:] = rope(qh[:, V7X_LANES:]).astype(BF16)
            k_nope = kv[:, hd * QK_PAD_DIM:hd * QK_PAD_DIM + QK_NOPE_DIM]
            ss = jnp.sum(k_nope * k_nope, axis=-1, keepdims=True) + kr_ss
            inv_k = lax.rsqrt(ss / QK_HEAD_DIM + EPS)
            k_ref[0, hd, rows, :V7X_LANES] = (k_nope * inv_k * kn[:, :V7X_LANES]).astype(BF16)
            k_ref[0, hd, rows, V7X_LANES:] = (kr_rot * inv_k).astype(BF16)
            v_ref[0, hd, rows, :] = kv[:, hd * QK_PAD_DIM + QK_NOPE_DIM:(hd + 1) * QK_PAD_DIM].astype(BF16)


def _pad_head_gain(gain):
    return jnp.concatenate([gain, jnp.zeros((QK_PAD_DIM - QK_HEAD_DIM,), F32)]).reshape(1, QK_PAD_DIM)


def _mla_proj(x, gain, w_down, q_a_norm, kv_a_norm, w_uq, w_ukv, q_norm, k_norm, cos_t, sinm_t):
    bsz, seq, d = x.shape
    tm = min(ROW_TILE, seq)
    n_down = w_down.shape[1]
    pad_down = (-n_down) % (2 * V7X_LANES)
    wdn = jnp.pad(w_down, ((0, 0), (0, pad_down))).astype(BF16)
    wuq = w_uq.reshape(Q_LORA_RANK, N_HEADS, QK_HEAD_DIM)
    wuq = jnp.pad(wuq, ((0, 0), (0, 0), (0, QK_PAD_DIM - QK_HEAD_DIM)))
    wuq = wuq.reshape(Q_LORA_RANK, N_HEADS * QK_PAD_DIM).astype(BF16)
    wukv = w_ukv.astype(BF16)
    row_spec = pl.BlockSpec((1, tm, d), lambda b, s: (b, s, 0))
    tab_spec = pl.BlockSpec((1, tm, V7X_LANES), lambda b, s: (b, s, 0))
    qk_spec = pl.BlockSpec((1, N_HEADS, tm, QK_PAD_DIM), lambda b, s: (b, 0, s, 0))
    v_spec = pl.BlockSpec((1, N_HEADS, tm, V_HEAD_DIM), lambda b, s: (b, 0, s, 0))
    qk_shape = jax.ShapeDtypeStruct((bsz, N_HEADS, seq, QK_PAD_DIM), BF16)
    v_shape = jax.ShapeDtypeStruct((bsz, N_HEADS, seq, V_HEAD_DIM), BF16)
    return pl.pallas_call(
        _mla_proj_kernel,
        grid=(bsz, seq // tm),
        in_specs=[
            row_spec,
            _resident((1, d)),
            _resident(wdn.shape),
            _resident((1, Q_LORA_RANK)),
            _resident((1, KV_LORA_RANK)),
            _resident(wuq.shape),
            _resident(wukv.shape),
            _resident((1, QK_PAD_DIM)),
            _resident((1, QK_PAD_DIM)),
            tab_spec,
            tab_spec,
        ],
        out_specs=[qk_spec, qk_spec, v_spec],
        out_shape=[qk_shape, qk_shape, v_shape],
        compiler_params=pltpu.CompilerParams(
            dimension_semantics=("arbitrary", "arbitrary"), vmem_limit_bytes=BIG_VMEM_LIMIT),
        name="mla_proj",
    )(x, gain.reshape(1, d), wdn, q_a_norm.reshape(1, -1), kv_a_norm.reshape(1, -1), wuq, wukv,
      _pad_head_gain(q_norm), _pad_head_gain(k_norm), cos_t, sinm_t)


def _attn_kernel(q_ref, k_ref, v_ref, o_ref):
    seq = q_ref.shape[2]
    t = min(ATTN_TILE, seq)
    dims = (((1,), (1,)), ((), ()))
    row = lax.broadcasted_iota(jnp.int32, (t, t), 0)
    col = lax.broadcasted_iota(jnp.int32, (t, t), 1)
    for qi in range(seq // t):
        q = q_ref[0, 0, qi * t:(qi + 1) * t, :]
        m = l = acc = None
        for ki in range(qi + 1):
            k = k_ref[0, 0, ki * t:(ki + 1) * t, :]
            v = v_ref[0, 0, ki * t:(ki + 1) * t, :]
            s = lax.dot_general(q, k, dims, preferred_element_type=F32)
            if ki == qi:
                s = jnp.where(col <= row, s, MASK_VALUE)
            s_max = jnp.max(s, axis=-1, keepdims=True)
            if ki == 0:
                m = s_max
                p = jnp.exp(s - m)
                l = jnp.sum(p, axis=-1, keepdims=True)
                acc = _dot(p.astype(BF16), v)
            else:
                m_new = jnp.maximum(m, s_max)
                alpha = jnp.exp(m - m_new)
                p = jnp.exp(s - m_new)
                l = alpha * l + jnp.sum(p, axis=-1, keepdims=True)
                acc = alpha * acc + _dot(p.astype(BF16), v)
                m = m_new
        o_ref[0, qi * t:(qi + 1) * t, :] = (acc / l).astype(o_ref.dtype)


def _attention(q, k, v):
    bsz, nh, seq, _ = q.shape
    dv = v.shape[-1]
    head_spec = lambda last: pl.BlockSpec((1, 1, seq, last), lambda b, h: (b, h, 0, 0))
    return pl.pallas_call(
        _attn_kernel,
        grid=(bsz, nh),
        in_specs=[head_spec(q.shape[-1]), head_spec(k.shape[-1]), head_spec(dv)],
        out_specs=pl.BlockSpec((1, seq, dv), lambda b, h: (b, 0, h)),
        out_shape=jax.ShapeDtypeStruct((bsz, seq, nh * dv), BF16),
        compiler_params=pltpu.CompilerParams(dimension_semantics=("arbitrary", "arbitrary")),
        name="causal_attention",
    )(q, k, v)


def _oproj_route_kernel(x_ref, o_ref, wo_ref, g_ref, wr_ref, xo_ref, h_ref, idx_ref, gate_ref):
    tm = x_ref.shape[0]
    x = x_ref[...] + _dot(o_ref[...], wo_ref[...])
    xo_ref[...] = x
    hn = _rms(x, g_ref[...])
    _store_row_tiles(h_ref, hn)
    logits = _dot(hn.astype(BF16), wr_ref[...])
    lane = lax.broadcasted_iota(jnp.int32, logits.shape, 1)
    logits = jnp.where(lane < N_EXPERTS, logits, MASK_VALUE)
    e = jnp.exp(logits - jnp.max(logits, axis=-1, keepdims=True))
    probs = e / jnp.sum(e, axis=-1, keepdims=True)
    probs = jnp.where(lane < N_EXPERTS, probs, -1.0)
    p1 = jnp.max(probs, axis=-1, keepdims=True)
    i1 = jnp.min(jnp.where(probs == p1, lane, V7X_LANES), axis=-1, keepdims=True)
    rest = jnp.where(lane == i1, -1.0, probs)
    p2 = jnp.max(rest, axis=-1, keepdims=True)
    i2 = jnp.min(jnp.where(rest == p2, lane, V7X_LANES), axis=-1, keepdims=True)
    denom = p1 + p2
    slot = lax.broadcasted_iota(jnp.int32, (tm, TOP_K), 1)
    idx_ref[...] = jnp.where(slot == 0, i1, i2)
    gate_ref[...] = jnp.where(slot == 0, p1 / denom, p2 / denom)


def _oproj_route(x2, o2, w_o, gain, router, layer):
    t, d = x2.shape
    assert d == V7X_SUBLANES * V7X_LANES, "row-DMA layout stores one token row per (8, 128) tile"
    tm = min(ROW_TILE, t)
    wr = jnp.pad(router, ((0, 0), (0, V7X_LANES - N_EXPERTS))).astype(BF16)
    return pl.pallas_call(
        _oproj_route_kernel,
        grid=(t // tm,),
        in_specs=[
            pl.BlockSpec((tm, d), lambda i: (i, 0)),
            pl.BlockSpec((tm, o2.shape[1]), lambda i: (i, 0)),
            _resident(w_o.shape, layer),
            _resident((1, d)),
            _resident(wr.shape),
        ],
        out_specs=[
            pl.BlockSpec((tm, d), lambda i: (i, 0)),
            pl.BlockSpec((tm * V7X_SUBLANES, V7X_LANES), lambda i: (i, 0)),
            pl.BlockSpec((tm, TOP_K), lambda i: (i, 0)),
            pl.BlockSpec((tm, TOP_K), lambda i: (i, 0)),
        ],
        out_shape=[
            jax.ShapeDtypeStruct((t, d), F32),
            jax.ShapeDtypeStruct((t * V7X_SUBLANES, V7X_LANES), F32),
            jax.ShapeDtypeStruct((t, TOP_K), jnp.int32),
            jax.ShapeDtypeStruct((t, TOP_K), F32),
        ],
        compiler_params=pltpu.CompilerParams(
            dimension_semantics=("arbitrary",), vmem_limit_bytes=BIG_VMEM_LIMIT),
        name="oproj_route",
    )(x2, o2, w_o, gain.reshape(1, d), wr)


def _rank_kernel(idx_ref, rank_ref, count_ref, carry_ref):
    tm = idx_ref.shape[0]

    @pl.when(pl.program_id(0) == 0)
    def _():
        carry_ref[...] = jnp.zeros_like(carry_ref)

    idx = idx_ref[...]
    i1 = idx[:, 0:1]
    i2 = idx[:, 1:2]
    lane = lax.broadcasted_iota(jnp.int32, (tm, V7X_LANES), 1)
    onehot = jnp.where((lane == i1) | (lane == i2), 1.0, 0.0)
    r = lax.broadcasted_iota(jnp.int32, (tm, tm), 0)
    c = lax.broadcasted_iota(jnp.int32, (tm, tm), 1)
    lower = jnp.where(c < r, 1.0, 0.0).astype(BF16)
    before = _dot(lower, onehot.astype(BF16)) + carry_ref[...]
    r1 = jnp.sum(jnp.where(lane == i1, before, 0.0), axis=-1, keepdims=True)
    r2 = jnp.sum(jnp.where(lane == i2, before, 0.0), axis=-1, keepdims=True)
    slot = lax.broadcasted_iota(jnp.int32, (tm, TOP_K), 1)
    rank_ref[...] = jnp.where(slot == 0, r1, r2).astype(jnp.int32)
    total = carry_ref[...] + jnp.sum(onehot, axis=0, keepdims=True)
    carry_ref[...] = total
    count_ref[...] = total.astype(jnp.int32)


def _route_ranks(idx):
    t = idx.shape[0]
    tm = min(ROW_TILE, t)
    return pl.pallas_call(
        _rank_kernel,
        grid=(t // tm,),
        in_specs=[pl.BlockSpec((tm, TOP_K), lambda i: (i, 0))],
        out_specs=[pl.BlockSpec((tm, TOP_K), lambda i: (i, 0)),
                   pl.BlockSpec((1, V7X_LANES), lambda i: (0, 0))],
        out_shape=[jax.ShapeDtypeStruct((t, TOP_K), jnp.int32),
                   jax.ShapeDtypeStruct((1, V7X_LANES), jnp.int32)],
        scratch_shapes=[pltpu.VMEM((1, V7X_LANES), F32)],
        compiler_params=pltpu.CompilerParams(dimension_semantics=("arbitrary",)),
        name="route_ranks",
    )(idx)


def _load_row_tiles(tiles_ref):
    rows = tiles_ref.shape[0] // V7X_SUBLANES
    return jnp.concatenate(
        [tiles_ref[pl.ds(c, rows, stride=V7X_SUBLANES), :] for c in range(V7X_SUBLANES)], axis=-1)


def _store_row_tiles(tiles_ref, value):
    rows = value.shape[0]
    for c in range(V7X_SUBLANES):
        tiles_ref[pl.ds(c, rows, stride=V7X_SUBLANES), :] = value[:, c * V7X_LANES:(c + 1) * V7X_LANES]


def _row_tile(ref, row):
    return ref.at[pl.ds(pl.multiple_of(row * V7X_SUBLANES, V7X_SUBLANES), V7X_SUBLANES), :]


def _dispatch_copies(h_ref, xs_ref, pos_ref, sem, group):
    base = pl.multiple_of(group * V7X_SUBLANES, V7X_SUBLANES)
    copies = []
    for u in range(V7X_SUBLANES):
        for k in range(TOP_K):
            dst_row = pos_ref[TOP_K * base + (TOP_K * u + k)]
            copies.append(pltpu.make_async_copy(
                _row_tile(h_ref, base + u), _row_tile(xs_ref, dst_row), sem))
    return copies


def _pad_tile_copies(meta_ref, zero_ref, xs_ref, sem):
    tile_sublanes = zero_ref.shape[0]
    tm = tile_sublanes // V7X_SUBLANES
    n_tiles = xs_ref.shape[0] // tile_sublanes

    def zero_tile(first_row):
        first = pl.multiple_of(first_row * V7X_SUBLANES, tile_sublanes)
        return pltpu.make_async_copy(zero_ref, xs_ref.at[pl.ds(first, tile_sublanes), :], sem)

    pairs = []
    for e in range(N_EXPERTS):
        end = meta_ref[e]
        pairs.append((end >= tm, zero_tile(jnp.maximum(end - tm, 0))))
    n_active = meta_ref[N_EXPERTS]
    for j in range(N_EXPERTS):
        tile = n_tiles - 1 - j
        pairs.append((tile >= n_active, zero_tile(tile * tm)))
    return pairs


def _dispatch_kernel(meta_ref, pos_ref, h_ref, xs_ref, zero_ref, zero_sem, sem):
    groups = h_ref.shape[0] // (V7X_SUBLANES * V7X_SUBLANES)

    @pl.when(pl.program_id(0) == 0)
    def _():
        zero_ref[...] = jnp.zeros_like(zero_ref)
        pairs = _pad_tile_copies(meta_ref, zero_ref, xs_ref, zero_sem)
        for cond, cp in pairs:
            pl.when(cond)(cp.start)
        for cond, cp in pairs:
            pl.when(cond)(cp.wait)

    def start(g, carry):
        for i, cp in enumerate(_dispatch_copies(h_ref, xs_ref, pos_ref, sem, g)):
            cp.start(priority=i % 2)
        return carry

    def wait(g, carry):
        for cp in _dispatch_copies(h_ref, xs_ref, pos_ref, sem, g):
            cp.wait()
        return carry

    lax.fori_loop(0, groups, start, 0)
    lax.fori_loop(0, groups, wait, 0)


def _dispatch(h3, pos_flat, meta, n_rows):
    t = h3.shape[0] // V7X_SUBLANES
    tm = min(MOVE_TILE, t)
    return pl.pallas_call(
        _dispatch_kernel,
        grid_spec=pltpu.PrefetchScalarGridSpec(
            num_scalar_prefetch=1,
            grid=(t // tm,),
            in_specs=[
                pl.BlockSpec((TOP_K * tm,), lambda i, meta: (i,), memory_space=pltpu.SMEM),
                pl.BlockSpec((tm * V7X_SUBLANES, V7X_LANES), lambda i, meta: (i, 0)),
            ],
            out_specs=pl.BlockSpec(memory_space=pl.ANY),
            scratch_shapes=[pltpu.VMEM((GROUP_TILE * V7X_SUBLANES, V7X_LANES), h3.dtype),
                            pltpu.SemaphoreType.DMA(()), pltpu.SemaphoreType.DMA(())],
        ),
        out_shape=jax.ShapeDtypeStruct((n_rows * V7X_SUBLANES, V7X_LANES), h3.dtype),
        compiler_params=pltpu.CompilerParams(dimension_semantics=("arbitrary",)),
        name="moe_dispatch",
    )(meta, pos_flat, h3)


def _experts_kernel(te_ref, meta_ref, x_ref, wg_ref, wu_ref, wd_ref, y_ref):
    del te_ref
    n_active = meta_ref[N_EXPERTS]

    @pl.when(pl.program_id(0) < n_active)
    def _():
        h = _load_row_tiles(x_ref).astype(BF16)
        gate = _dot(h, wg_ref[...])
        up = _dot(h, wu_ref[...])
        act = (gate * jax.nn.sigmoid(gate) * up).astype(BF16)
        _store_row_tiles(y_ref, _dot(act, wd_ref[...]))

    @pl.when(pl.program_id(0) >= n_active)
    def _():
        y_ref[...] = jnp.zeros_like(y_ref)


def _experts(xs, tile_expert, meta, wg, wu, wd, layer):
    tile_sublanes = GROUP_TILE * V7X_SUBLANES
    d, f = wg.shape[-2:]

    def row_map(i, te, meta):
        return (jnp.minimum(i, meta[N_EXPERTS] - 1), 0)

    def w_map(i, te, meta):
        return (layer, te[jnp.minimum(i, meta[N_EXPERTS] - 1)], 0, 0)

    return pl.pallas_call(
        _experts_kernel,
        grid_spec=pltpu.PrefetchScalarGridSpec(
            num_scalar_prefetch=2,
            grid=(xs.shape[0] // tile_sublanes,),
            in_specs=[
                pl.BlockSpec((tile_sublanes, V7X_LANES), row_map),
                pl.BlockSpec((None, None, d, f), w_map),
                pl.BlockSpec((None, None, d, f), w_map),
                pl.BlockSpec((None, None, f, d), w_map),
            ],
            out_specs=pl.BlockSpec((tile_sublanes, V7X_LANES), lambda i, te, meta: (i, 0)),
        ),
        out_shape=jax.ShapeDtypeStruct(xs.shape, F32),
        compiler_params=pltpu.CompilerParams(
            dimension_semantics=("arbitrary",), vmem_limit_bytes=BIG_VMEM_LIMIT),
        name="moe_experts",
    )(tile_expert, meta, xs, wg, wu, wd)


def _combine_copies(ys_ref, buf_ref, pos_ref, sem, group):
    base = pl.multiple_of(group * V7X_SUBLANES, V7X_SUBLANES)
    copies = []
    for u in range(V7X_SUBLANES):
        for k in range(TOP_K):
            src_row = pos_ref[TOP_K * base + (TOP_K * u + k)]
            copies.append(pltpu.make_async_copy(
                _row_tile(ys_ref, src_row), _row_tile(buf_ref.at[k], base + u), sem))
    return copies


def _combine_kernel(pos_ref, x_ref, gate_ref, ys_ref, o_ref, buf_ref, sem):
    groups = x_ref.shape[0] // V7X_SUBLANES

    def start(g, carry):
        for i, cp in enumerate(_combine_copies(ys_ref, buf_ref, pos_ref, sem, g)):
            cp.start(priority=i % 2)
        return carry

    def wait(g, carry):
        for cp in _combine_copies(ys_ref, buf_ref, pos_ref, sem, g):
            cp.wait()
        return carry

    lax.fori_loop(0, groups, start, 0)
    lax.fori_loop(0, groups, wait, 0)
    gate = gate_ref[...]
    o_ref[...] = x_ref[...] + (gate[:, 0:1] * _load_row_tiles(buf_ref.at[0])
                               + gate[:, 1:2] * _load_row_tiles(buf_ref.at[1]))


def _combine(x2, gate, ys, pos_flat):
    t, d = x2.shape
    tm = min(MOVE_TILE, t)
    return pl.pallas_call(
        _combine_kernel,
        grid=(t // tm,),
        in_specs=[
            pl.BlockSpec((TOP_K * tm,), lambda i: (i,), memory_space=pltpu.SMEM),
            pl.BlockSpec((tm, d), lambda i: (i, 0)),
            pl.BlockSpec((tm, TOP_K), lambda i: (i, 0)),
            pl.BlockSpec(memory_space=pl.ANY),
        ],
        out_specs=pl.BlockSpec((tm, d), lambda i: (i, 0)),
        out_shape=jax.ShapeDtypeStruct((t, d), F32),
        scratch_shapes=[pltpu.VMEM((TOP_K, tm * V7X_SUBLANES, V7X_LANES), F32),
                        pltpu.SemaphoreType.DMA(())],
        compiler_params=pltpu.CompilerParams(dimension_semantics=("arbitrary",)),
        name="moe_combine",
    )(pos_flat, x2, gate, ys)


def _moe(x2, h3, idx, gate, wg, wu, wd, layer):
    t = x2.shape[0]
    tm = GROUP_TILE
    rank, counts = _route_ranks(idx)
    counts = counts[0, :N_EXPERTS]
    padded = (counts + tm - 1) // tm * tm
    ends = jnp.cumsum(padded)
    starts = ends - padded
    pos = starts[idx] + rank
    n_tiles = TOP_K * t // tm + N_EXPERTS
    tile_start = jnp.arange(n_tiles, dtype=jnp.int32) * tm
    tile_expert = jnp.minimum(
        jnp.sum((tile_start[:, None] >= ends[None, :]).astype(jnp.int32), axis=1), N_EXPERTS - 1)
    meta = jnp.concatenate([ends, ends[-1:] // tm]).astype(jnp.int32)
    pos_flat = pos.reshape(-1).astype(jnp.int32)
    xs = _dispatch(h3, pos_flat, meta, n_tiles * tm)
    ys = _experts(xs, tile_expert.astype(jnp.int32), meta, wg, wu, wd, layer)
    return _combine(x2, gate, ys, pos_flat)


def kernel(x, positions, norm_mix, norm_ffn, conv_w_in, conv_w, conv_w_out, mla_w_down, mla_q_a_norm, mla_kv_a_norm, mla_w_uq, mla_w_ukv, mla_q_norm, mla_k_norm, mla_w_o, ffn_w_gate, ffn_w_up, ffn_w_down, moe_router, moe_w_gate, moe_w_up, moe_w_down):
    bsz, seq, d = x.shape
    depth = norm_mix.shape[0]
    cos_t, sinm_t = _rope_tables(positions)
    conv_w_in, conv_w_out, mla_w_o = (w.astype(BF16) for w in (conv_w_in, conv_w_out, mla_w_o))
    ffn_w_gate, ffn_w_up, ffn_w_down = (w.astype(BF16) for w in (ffn_w_gate, ffn_w_up, ffn_w_down))
    moe_w_gate, moe_w_up, moe_w_down = (w.astype(BF16) for w in (moe_w_gate, moe_w_up, moe_w_down))
    for i in range(depth):
        j = i // 2
        if i % 2 == 0:
            x = _conv_mixer(x, norm_mix[i], conv_w_in, conv_w, conv_w_out, j)
            x = _dense_ffn(x.reshape(bsz * seq, d), norm_ffn[i], ffn_w_gate, ffn_w_up, ffn_w_down,
                           j).reshape(bsz, seq, d)
        else:
            q, k, v = _mla_proj(x, norm_mix[i], mla_w_down[j], mla_q_a_norm[j], mla_kv_a_norm[j],
                                mla_w_uq[j], mla_w_ukv[j], mla_q_norm[j], mla_k_norm[j], cos_t, sinm_t)
            o = _attention(q, k, v)
            x2, h3, idx, gate = _oproj_route(
                x.reshape(bsz * seq, d), o.reshape(bsz * seq, -1), mla_w_o, norm_ffn[i], moe_router[j], j)
            x = _moe(x2, h3, idx, gate, moe_w_gate, moe_w_up, moe_w_down, j).reshape(bsz, seq, d)
    return x
```

```python
import jax
import jax.numpy as jnp
from jax import lax
from jax.experimental import pallas as pl
from jax.experimental.pallas import tpu as pltpu

N_HEADS = 8
QK_NOPE_DIM = 128
QK_ROPE_DIM = 64
QK_HEAD_DIM = QK_NOPE_DIM + QK_ROPE_DIM
V_HEAD_DIM = 128
Q_LORA_RANK = 384
KV_LORA_RANK = 256
ROPE_THETA = 10000.0
N_EXPERTS = 8
TOP_K = 2
EPS = 1e-6

V7X_LANES = 128
V7X_SUBLANES = 8
V7X_VMEM_BYTES = 64 * 1024 * 1024
BIG_VMEM_LIMIT = V7X_VMEM_BYTES - 8 * 1024 * 1024

QK_PAD_DIM = 2 * V7X_LANES
Q_HEAD_COLS = 3 * V7X_LANES
MASK_VALUE = -1e30

ROW_TILE = 512
SUB_TILE = 256
ATTN_TILE = 256
GROUP_TILE = 512
MOVE_TILE = 512

F32 = jnp.float32
BF16 = jnp.bfloat16


def _rms(x, gain):
    return x * lax.rsqrt(jnp.mean(x * x, axis=-1, keepdims=True) + EPS) * gain


def _dot(a, b):
    return jnp.dot(a, b, preferred_element_type=F32)


def _resident(shape, layer=None):
    nd = len(shape)
    if layer is None:
        return pl.BlockSpec(shape, lambda *_: (0,) * nd, pipeline_mode=pl.Buffered(1))
    return pl.BlockSpec((None,) + tuple(shape[1:]), lambda *_: (layer,) + (0,) * (nd - 1),
                        pipeline_mode=pl.Buffered(1))


def _conv_kernel(x_ref, g_ref, w_in_ref, cw_ref, w_out_ref, o_ref, carry_ref):
    d = x_ref.shape[-1]
    tm = x_ref.shape[1]

    @pl.when(pl.program_id(1) == 0)
    def _():
        carry_ref[...] = jnp.zeros_like(carry_ref)

    x = x_ref[0]
    h = _rms(x, g_ref[...]).astype(BF16)
    y = _dot(h, w_in_ref[...])
    b_gate = y[:, :d]
    v = y[:, d:2 * d] * y[:, 2 * d:]
    prev = carry_ref[...]
    p1 = prev[V7X_SUBLANES - 1:V7X_SUBLANES]
    p2 = prev[V7X_SUBLANES - 2:V7X_SUBLANES - 1]
    rows = lax.broadcasted_iota(jnp.int32, (tm, d), 0)
    v1 = jnp.where(rows == 0, p1, pltpu.roll(v, 1, 0))
    v2 = jnp.where(rows == 0, p2, jnp.where(rows == 1, p1, pltpu.roll(v, 2, 0)))
    carry_ref[...] = v[tm - V7X_SUBLANES:]
    cw = cw_ref[...]
    conv = v2 * cw[0:1] + v1 * cw[1:2] + v * cw[2:3]
    z = (b_gate * conv).astype(BF16)
    o_ref[0] = x + _dot(z, w_out_ref[...])


def _conv_mixer(x, gain, w_in, conv_w, w_out, layer):
    bsz, seq, d = x.shape
    tm = min(ROW_TILE, seq)
    return pl.pallas_call(
        _conv_kernel,
        grid=(bsz, seq // tm),
        in_specs=[
            pl.BlockSpec((1, tm, d), lambda b, s: (b, s, 0)),
            _resident((1, d)),
            _resident(w_in.shape, layer),
            _resident(conv_w.shape, layer),
            _resident(w_out.shape, layer),
        ],
        out_specs=pl.BlockSpec((1, tm, d), lambda b, s: (b, s, 0)),
        out_shape=jax.ShapeDtypeStruct(x.shape, F32),
        scratch_shapes=[pltpu.VMEM((V7X_SUBLANES, d), F32)],
        compiler_params=pltpu.CompilerParams(
            dimension_semantics=("arbitrary", "arbitrary"), vmem_limit_bytes=BIG_VMEM_LIMIT),
        name="conv_mixer",
    )(x, gain.reshape(1, d), w_in, conv_w, w_out)


def _ffn_kernel(x_ref, g_ref, wg_ref, wu_ref, wd_ref, o_ref):
    x = x_ref[...]
    h = _rms(x, g_ref[...]).astype(BF16)
    gate = _dot(h, wg_ref[...])
    up = _dot(h, wu_ref[...])
    act = (gate * jax.nn.sigmoid(gate) * up).astype(BF16)
    o_ref[...] = x + _dot(act, wd_ref[...])


def _dense_ffn(x2, gain, wg, wu, wd, layer):
    t, d = x2.shape
    tm = min(ROW_TILE, t)
    return pl.pallas_call(
        _ffn_kernel,
        grid=(t // tm,),
        in_specs=[
            pl.BlockSpec((tm, d), lambda i: (i, 0)),
            _resident((1, d)),
            _resident(wg.shape, layer),
            _resident(wu.shape, layer),
            _resident(wd.shape, layer),
        ],
        out_specs=pl.BlockSpec((tm, d), lambda i: (i, 0)),
        out_shape=jax.ShapeDtypeStruct(x2.shape, F32),
        compiler_params=pltpu.CompilerParams(
            dimension_semantics=("arbitrary",), vmem_limit_bytes=BIG_VMEM_LIMIT),
        name="dense_ffn",
    )(x2, gain.reshape(1, d), wg, wu, wd)


def _rope_table_kernel(pos_ref, inv_ref, cos_ref, sin_ref, nsin_ref):
    ang = pos_ref[...] * inv_ref[...]
    s = jnp.sin(ang)
    cos_ref[...] = jnp.cos(ang)
    sin_ref[...] = s
    nsin_ref[...] = -s


def _rope_tables(positions):
    bsz, seq = positions.shape
    half = QK_ROPE_DIM // 2
    t = bsz * seq
    inv_freq = ROPE_THETA ** (-jnp.arange(0, QK_ROPE_DIM, 2, dtype=F32) / QK_ROPE_DIM)
    per_row = V7X_LANES // half
    rows = t // per_row
    pos_rep = jnp.repeat(positions.reshape(-1).astype(F32), half).reshape(rows, V7X_LANES)
    inv_rep = jnp.tile(inv_freq, per_row).reshape(1, V7X_LANES)
    tr = min(1024, rows)
    shp = jax.ShapeDtypeStruct((rows, V7X_LANES), F32)
    cos, sin, nsin = pl.pallas_call(
        _rope_table_kernel,
        grid=(rows // tr,),
        in_specs=[pl.BlockSpec((tr, V7X_LANES), lambda i: (i, 0)),
                  pl.BlockSpec((1, V7X_LANES), lambda i: (0, 0))],
        out_specs=[pl.BlockSpec((tr, V7X_LANES), lambda i: (i, 0))] * 3,
        out_shape=[shp, shp, shp],
        name="rope_tables",
    )(pos_rep, inv_rep)
    cos, sin, nsin = (a.reshape(bsz, seq, half) for a in (cos, sin, nsin))
    zeros = jnp.zeros((bsz, seq, V7X_LANES - QK_ROPE_DIM), F32)
    cos_t = jnp.concatenate([cos, cos, zeros], axis=-1)
    sinm_t = jnp.concatenate([nsin, sin, zeros], axis=-1)
    return cos_t, sinm_t


def _mla_proj_kernel(x_ref, g_ref, wdn_ref, qa_ref, kva_ref, wuq_ref, wukv_ref,
                     qn_ref, kn_ref, cos_ref, sinm_ref, q_ref, k_ref, v_ref):
    qn = qn_ref[...]
    kn = kn_ref[...]
    scale = QK_HEAD_DIM ** -0.5
    tm = x_ref.shape[1]
    sub = min(SUB_TILE, tm)
    lane = lax.broadcasted_iota(jnp.int32, (1, V7X_LANES), 1)
    rope_lanes = lane < QK_ROPE_DIM
    r = lax.broadcasted_iota(jnp.int32, (QK_PAD_DIM, QK_PAD_DIM), 0)
    c = lax.broadcasted_iota(jnp.int32, (QK_PAD_DIM, QK_PAD_DIM), 1)
    pair_ones = jnp.where((r < V7X_LANES) == (c < V7X_LANES), 1.0, 0.0).astype(BF16)

    def pair_sums(a, b):
        sums = _dot(jnp.concatenate([a, b], axis=-1).astype(BF16), pair_ones)
        return sums[:, :V7X_LANES], sums[:, V7X_LANES:]

    for r0 in range(0, tm, sub):
        rows = slice(r0, r0 + sub)
        h = _rms(x_ref[0, rows, :], g_ref[...]).astype(BF16)
        down = _dot(h, wdn_ref[...])
        c_q = _rms(down[:, :Q_LORA_RANK], qa_ref[...]).astype(BF16)
        c_kv = _rms(down[:, Q_LORA_RANK:Q_LORA_RANK + KV_LORA_RANK], kva_ref[...]).astype(BF16)
        k_rope = down[:, Q_LORA_RANK + KV_LORA_RANK:]
        q = _dot(c_q, wuq_ref[...])
        kv = _dot(c_kv, wukv_ref[...])
        cos = cos_ref[0, rows, :]
        sinm = sinm_ref[0, rows, :]
        kr = k_rope * kn[:, V7X_LANES:]
        kr_rot = kr * cos + pltpu.roll(kr, QK_ROPE_DIM, 1) * sinm
        kr_sq = jnp.where(rope_lanes, k_rope * k_rope, 0.0)
        for h0 in range(0, N_HEADS, 2):
            heads = (h0, h0 + 1)
            q3 = [q[:, hd * Q_HEAD_COLS:(hd + 1) * Q_HEAD_COLS] for hd in heads]
            k_nope = [kv[:, hd * QK_PAD_DIM:hd * QK_PAD_DIM + QK_NOPE_DIM] for hd in heads]
            q_ss = pair_sums(*[t[:, :V7X_LANES] * t[:, :V7X_LANES]
                               + t[:, V7X_LANES:QK_PAD_DIM] * t[:, V7X_LANES:QK_PAD_DIM] for t in q3])
            k_ss = pair_sums(*[t * t + kr_sq for t in k_nope])
            for i, hd in enumerate(heads):
                inv_q = lax.rsqrt(q_ss[i] / QK_HEAD_DIM + EPS) * scale
                qh = [q3[i][:, j * V7X_LANES:(j + 1) * V7X_LANES] * inv_q
                      * qn[:, j * V7X_LANES:(j + 1) * V7X_LANES] for j in range(3)]
                q_ref[0, hd, rows, :V7X_LANES] = qh[0].astype(BF16)
                q_ref[0, hd, rows, V7X_LANES:] = (qh[1] * cos + qh[2] * sinm).astype(BF16)
                inv_k = lax.rsqrt(k_ss[i] / QK_HEAD_DIM + EPS)
                k_ref[0, hd, rows, :V7X_LANES] = (k_nope[i] * inv_k * kn[:, :V7X_LANES]).astype(BF16)
                k_ref[0, hd, rows, V7X_LANES:] = (kr_rot * inv_k).astype(BF16)
                v_ref[0, hd, rows, :] = kv[:, hd * QK_PAD_DIM + QK_NOPE_DIM:(hd + 1) * QK_PAD_DIM].astype(BF16)


def _swap_halves(a):
    half = QK_ROPE_DIM // 2
    return jnp.concatenate([a[..., half:], a[..., :half]], axis=-1)


def _mla_proj(x, gain, w_down, q_a_norm, kv_a_norm, w_uq, w_ukv, q_norm, k_norm, cos_t, sinm_t):
    bsz, seq, d = x.shape
    tm = min(ROW_TILE, seq)
    lane_pad = V7X_LANES - QK_ROPE_DIM
    w_rope = w_down[:, Q_LORA_RANK + KV_LORA_RANK:]
    wdn = jnp.concatenate([w_down, _swap_halves(w_rope)], axis=1).astype(BF16)
    wuq = w_uq.reshape(Q_LORA_RANK, N_HEADS, QK_HEAD_DIM)
    wq_rope = wuq[..., QK_NOPE_DIM:]
    zeros = jnp.zeros((Q_LORA_RANK, N_HEADS, lane_pad), w_uq.dtype)
    wuq = jnp.concatenate([wuq[..., :QK_NOPE_DIM], wq_rope, zeros, _swap_halves(wq_rope), zeros], axis=-1)
    wuq = wuq.reshape(Q_LORA_RANK, N_HEADS * Q_HEAD_COLS).astype(BF16)
    wukv = w_ukv.astype(BF16)
    gz = jnp.zeros((lane_pad,), F32)
    qn = jnp.concatenate([q_norm[:QK_NOPE_DIM], q_norm[QK_NOPE_DIM:], gz,
                          _swap_halves(q_norm[QK_NOPE_DIM:]), gz]).reshape(1, Q_HEAD_COLS)
    kn = jnp.concatenate([k_norm, _swap_halves(k_norm[QK_NOPE_DIM:])]).reshape(1, QK_PAD_DIM)
    row_spec = pl.BlockSpec((1, tm, d), lambda b, s: (b, s, 0))
    tab_spec = pl.BlockSpec((1, tm, V7X_LANES), lambda b, s: (b, s, 0))
    qk_spec = pl.BlockSpec((1, N_HEADS, tm, QK_PAD_DIM), lambda b, s: (b, 0, s, 0))
    v_spec = pl.BlockSpec((1, N_HEADS, tm, V_HEAD_DIM), lambda b, s: (b, 0, s, 0))
    qk_shape = jax.ShapeDtypeStruct((bsz, N_HEADS, seq, QK_PAD_DIM), BF16)
    v_shape = jax.ShapeDtypeStruct((bsz, N_HEADS, seq, V_HEAD_DIM), BF16)
    return pl.pallas_call(
        _mla_proj_kernel,
        grid=(bsz, seq // tm),
        in_specs=[
            row_spec,
            _resident((1, d)),
            _resident(wdn.shape),
            _resident((1, Q_LORA_RANK)),
            _resident((1, KV_LORA_RANK)),
            _resident(wuq.shape),
            _resident(wukv.shape),
            _resident(qn.shape),
            _resident(kn.shape),
            tab_spec,
            tab_spec,
        ],
        out_specs=[qk_spec, qk_spec, v_spec],
        out_shape=[qk_shape, qk_shape, v_shape],
        compiler_params=pltpu.CompilerParams(
            dimension_semantics=("arbitrary", "arbitrary"), vmem_limit_bytes=BIG_VMEM_LIMIT),
        name="mla_proj",
    )(x, gain.reshape(1, d), wdn, q_a_norm.reshape(1, -1), kv_a_norm.reshape(1, -1), wuq, wukv,
      qn, kn, cos_t, sinm_t)


def _attn_kernel(q_ref, k_ref, v_ref, o_ref):
    seq = q_ref.shape[2]
    t = min(ATTN_TILE, seq)
    dims = (((1,), (1,)), ((), ()))
    row = lax.broadcasted_iota(jnp.int32, (t, t), 0)
    col = lax.broadcasted_iota(jnp.int32, (t, t), 1)
    for qi in range(seq // t):
        q = q_ref[0, 0, qi * t:(qi + 1) * t, :]
        m = l = acc = None
        for ki in range(qi + 1):
            k = k_ref[0, 0, ki * t:(ki + 1) * t, :]
            v = v_ref[0, 0, ki * t:(ki + 1) * t, :]
            s = lax.dot_general(q, k, dims, preferred_element_type=F32)
            if ki == qi:
                s = jnp.where(col <= row, s, MASK_VALUE)
            s_max = jnp.max(s, axis=-1, keepdims=True)
            if ki == 0:
                m = s_max
                p = jnp.exp(s - m)
                l = jnp.sum(p, axis=-1, keepdims=True)
                acc = _dot(p.astype(BF16), v)
            else:
                m_new = jnp.maximum(m, s_max)
                alpha = jnp.exp(m - m_new)
                p = jnp.exp(s - m_new)
                l = alpha * l + jnp.sum(p, axis=-1, keepdims=True)
                acc = alpha * acc + _dot(p.astype(BF16), v)
                m = m_new
        o_ref[0, qi * t:(qi + 1) * t, :] = (acc / l).astype(o_ref.dtype)


def _attention(q, k, v):
    bsz, nh, seq, _ = q.shape
    dv = v.shape[-1]
    head_spec = lambda last: pl.BlockSpec((1, 1, seq, last), lambda b, h: (b, h, 0, 0))
    return pl.pallas_call(
        _attn_kernel,
        grid=(bsz, nh),
        in_specs=[head_spec(q.shape[-1]), head_spec(k.shape[-1]), head_spec(dv)],
        out_specs=pl.BlockSpec((1, seq, dv), lambda b, h: (b, 0, h)),
        out_shape=jax.ShapeDtypeStruct((bsz, seq, nh * dv), BF16),
        compiler_params=pltpu.CompilerParams(dimension_semantics=("arbitrary", "arbitrary")),
        name="causal_attention",
    )(q, k, v)


def _oproj_route_kernel(x_ref, o_ref, wo_ref, g_ref, wr_ref, xo_ref, h_ref, idx_ref, gate_ref):
    tm = x_ref.shape[0]
    x = x_ref[...] + _dot(o_ref[...], wo_ref[...])
    xo_ref[...] = x
    hn = _rms(x, g_ref[...])
    _store_row_tiles(h_ref, hn)
    logits = _dot(hn.astype(BF16), wr_ref[...])
    lane = lax.broadcasted_iota(jnp.int32, logits.shape, 1)
    logits = jnp.where(lane < N_EXPERTS, logits, MASK_VALUE)
    e = jnp.exp(logits - jnp.max(logits, axis=-1, keepdims=True))
    probs = e / jnp.sum(e, axis=-1, keepdims=True)
    probs = jnp.where(lane < N_EXPERTS, probs, -1.0)
    p1 = jnp.max(probs, axis=-1, keepdims=True)
    i1 = jnp.min(jnp.where(probs == p1, lane, V7X_LANES), axis=-1, keepdims=True)
    rest = jnp.where(lane == i1, -1.0, probs)
    p2 = jnp.max(rest, axis=-1, keepdims=True)
    i2 = jnp.min(jnp.where(rest == p2, lane, V7X_LANES), axis=-1, keepdims=True)
    denom = p1 + p2
    slot = lax.broadcasted_iota(jnp.int32, (tm, TOP_K), 1)
    idx_ref[...] = jnp.where(slot == 0, i1, i2)
    gate_ref[...] = jnp.where(slot == 0, p1 / denom, p2 / denom)


def _oproj_route(x2, o2, w_o, gain, router, layer):
    t, d = x2.shape
    assert d == V7X_SUBLANES * V7X_LANES, "row-DMA layout stores one token row per (8, 128) tile"
    tm = min(ROW_TILE, t)
    wr = jnp.pad(router, ((0, 0), (0, V7X_LANES - N_EXPERTS))).astype(BF16)
    return pl.pallas_call(
        _oproj_route_kernel,
        grid=(t // tm,),
        in_specs=[
            pl.BlockSpec((tm, d), lambda i: (i, 0)),
            pl.BlockSpec((tm, o2.shape[1]), lambda i: (i, 0)),
            _resident(w_o.shape, layer),
            _resident((1, d)),
            _resident(wr.shape),
        ],
        out_specs=[
            pl.BlockSpec((tm, d), lambda i: (i, 0)),
            pl.BlockSpec((tm * V7X_SUBLANES, V7X_LANES), lambda i: (i, 0)),
            pl.BlockSpec((tm, TOP_K), lambda i: (i, 0)),
            pl.BlockSpec((tm, TOP_K), lambda i: (i, 0)),
        ],
        out_shape=[
            jax.ShapeDtypeStruct((t, d), F32),
            jax.ShapeDtypeStruct((t * V7X_SUBLANES, V7X_LANES), F32),
            jax.ShapeDtypeStruct((t, TOP_K), jnp.int32),
            jax.ShapeDtypeStruct((t, TOP_K), F32),
        ],
        compiler_params=pltpu.CompilerParams(
            dimension_semantics=("arbitrary",), vmem_limit_bytes=BIG_VMEM_LIMIT),
        name="oproj_route",
    )(x2, o2, w_o, gain.reshape(1, d), wr)


def _rank_kernel(idx_ref, rank_ref, count_ref, carry_ref):
    tm = idx_ref.shape[0]

    @pl.when(pl.program_id(0) == 0)
    def _():
        carry_ref[...] = jnp.zeros_like(carry_ref)

    idx = idx_ref[...]
    i1 = idx[:, 0:1]
    i2 = idx[:, 1:2]
    lane = lax.broadcasted_iota(jnp.int32, (tm, V7X_LANES), 1)
    onehot = jnp.where((lane == i1) | (lane == i2), 1.0, 0.0)
    r = lax.broadcasted_iota(jnp.int32, (tm, tm), 0)
    c = lax.broadcasted_iota(jnp.int32, (tm, tm), 1)
    lower = jnp.where(c < r, 1.0, 0.0).astype(BF16)
    before = _dot(lower, onehot.astype(BF16)) + carry_ref[...]
    r1 = jnp.sum(jnp.where(lane == i1, before, 0.0), axis=-1, keepdims=True)
    r2 = jnp.sum(jnp.where(lane == i2, before, 0.0), axis=-1, keepdims=True)
    slot = lax.broadcasted_iota(jnp.int32, (tm, TOP_K), 1)
    rank_ref[...] = jnp.where(slot == 0, r1, r2).astype(jnp.int32)
    total = carry_ref[...] + jnp.sum(onehot, axis=0, keepdims=True)
    carry_ref[...] = total
    count_ref[...] = total.astype(jnp.int32)


def _route_ranks(idx):
    t = idx.shape[0]
    tm = min(ROW_TILE, t)
    return pl.pallas_call(
        _rank_kernel,
        grid=(t // tm,),
        in_specs=[pl.BlockSpec((tm, TOP_K), lambda i: (i, 0))],
        out_specs=[pl.BlockSpec((tm, TOP_K), lambda i: (i, 0)),
                   pl.BlockSpec((1, V7X_LANES), lambda i: (0, 0))],
        out_shape=[jax.ShapeDtypeStruct((t, TOP_K), jnp.int32),
                   jax.ShapeDtypeStruct((1, V7X_LANES), jnp.int32)],
        scratch_shapes=[pltpu.VMEM((1, V7X_LANES), F32)],
        compiler_params=pltpu.CompilerParams(dimension_semantics=("arbitrary",)),
        name="route_ranks",
    )(idx)


def _load_row_tiles(tiles_ref):
    rows = tiles_ref.shape[0] // V7X_SUBLANES
    return jnp.concatenate(
        [tiles_ref[pl.ds(c, rows, stride=V7X_SUBLANES), :] for c in range(V7X_SUBLANES)], axis=-1)


def _store_row_tiles(tiles_ref, value):
    rows = value.shape[0]
    for c in range(V7X_SUBLANES):
        tiles_ref[pl.ds(c, rows, stride=V7X_SUBLANES), :] = value[:, c * V7X_LANES:(c + 1) * V7X_LANES]


def _row_tile(ref, row):
    return ref.at[pl.ds(pl.multiple_of(row * V7X_SUBLANES, V7X_SUBLANES), V7X_SUBLANES), :]


def _dispatch_copies(h_ref, xs_ref, pos_ref, sem, group):
    base = pl.multiple_of(group * V7X_SUBLANES, V7X_SUBLANES)
    copies = []
    for u in range(V7X_SUBLANES):
        for k in range(TOP_K):
            dst_row = pos_ref[TOP_K * base + (TOP_K * u + k)]
            copies.append(pltpu.make_async_copy(
                _row_tile(h_ref, base + u), _row_tile(xs_ref, dst_row), sem))
    return copies


def _pad_tile_copies(meta_ref, zero_ref, xs_ref, sem):
    tile_sublanes = zero_ref.shape[0]
    tm = tile_sublanes // V7X_SUBLANES
    n_tiles = xs_ref.shape[0] // tile_sublanes

    def zero_tile(first_row):
        first = pl.multiple_of(first_row * V7X_SUBLANES, tile_sublanes)
        return pltpu.make_async_copy(zero_ref, xs_ref.at[pl.ds(first, tile_sublanes), :], sem)

    pairs = []
    for e in range(N_EXPERTS):
        end = meta_ref[e]
        pairs.append((end >= tm, zero_tile(jnp.maximum(end - tm, 0))))
    n_active = meta_ref[N_EXPERTS]
    for j in range(N_EXPERTS):
        tile = n_tiles - 1 - j
        pairs.append((tile >= n_active, zero_tile(tile * tm)))
    return pairs


def _dispatch_kernel(meta_ref, pos_ref, h_ref, xs_ref, zero_ref, zero_sem, sem):
    groups = h_ref.shape[0] // (V7X_SUBLANES * V7X_SUBLANES)

    @pl.when(pl.program_id(0) == 0)
    def _():
        zero_ref[...] = jnp.zeros_like(zero_ref)
        pairs = _pad_tile_copies(meta_ref, zero_ref, xs_ref, zero_sem)
        for cond, cp in pairs:
            pl.when(cond)(cp.start)
        for cond, cp in pairs:
            pl.when(cond)(cp.wait)

    def start(g, carry):
        for i, cp in enumerate(_dispatch_copies(h_ref, xs_ref, pos_ref, sem, g)):
            cp.start(priority=i % 2)
        return carry

    def wait(g, carry):
        for cp in _dispatch_copies(h_ref, xs_ref, pos_ref, sem, g):
            cp.wait()
        return carry

    lax.fori_loop(0, groups, start, 0)
    lax.fori_loop(0, groups, wait, 0)


def _dispatch(h3, pos_flat, meta, n_rows):
    t = h3.shape[0] // V7X_SUBLANES
    tm = min(MOVE_TILE, t)
    return pl.pallas_call(
        _dispatch_kernel,
        grid_spec=pltpu.PrefetchScalarGridSpec(
            num_scalar_prefetch=1,
            grid=(t // tm,),
            in_specs=[
                pl.BlockSpec((TOP_K * tm,), lambda i, meta: (i,), memory_space=pltpu.SMEM),
                pl.BlockSpec((tm * V7X_SUBLANES, V7X_LANES), lambda i, meta: (i, 0)),
            ],
            out_specs=pl.BlockSpec(memory_space=pl.ANY),
            scratch_shapes=[pltpu.VMEM((GROUP_TILE * V7X_SUBLANES, V7X_LANES), h3.dtype),
                            pltpu.SemaphoreType.DMA(()), pltpu.SemaphoreType.DMA(())],
        ),
        out_shape=jax.ShapeDtypeStruct((n_rows * V7X_SUBLANES, V7X_LANES), h3.dtype),
        compiler_params=pltpu.CompilerParams(dimension_semantics=("arbitrary",)),
        name="moe_dispatch",
    )(meta, pos_flat, h3)


def _experts_kernel(te_ref, meta_ref, x_ref, wg_ref, wu_ref, wd_ref, y_ref):
    del te_ref
    n_active = meta_ref[N_EXPERTS]

    @pl.when(pl.program_id(0) < n_active)
    def _():
        h = _load_row_tiles(x_ref).astype(BF16)
        gate = _dot(h, wg_ref[...])
        up = _dot(h, wu_ref[...])
        act = (gate * jax.nn.sigmoid(gate) * up).astype(BF16)
        _store_row_tiles(y_ref, _dot(act, wd_ref[...]))

    @pl.when(pl.program_id(0) >= n_active)
    def _():
        y_ref[...] = jnp.zeros_like(y_ref)


def _experts(xs, tile_expert, meta, wg, wu, wd, layer):
    tile_sublanes = GROUP_TILE * V7X_SUBLANES
    d, f = wg.shape[-2:]

    def row_map(i, te, meta):
        return (jnp.minimum(i, meta[N_EXPERTS] - 1), 0)

    def w_map(i, te, meta):
        return (layer, te[jnp.minimum(i, meta[N_EXPERTS] - 1)], 0, 0)

    return pl.pallas_call(
        _experts_kernel,
        grid_spec=pltpu.PrefetchScalarGridSpec(
            num_scalar_prefetch=2,
            grid=(xs.shape[0] // tile_sublanes,),
            in_specs=[
                pl.BlockSpec((tile_sublanes, V7X_LANES), row_map),
                pl.BlockSpec((None, None, d, f), w_map),
                pl.BlockSpec((None, None, d, f), w_map),
                pl.BlockSpec((None, None, f, d), w_map),
            ],
            out_specs=pl.BlockSpec((tile_sublanes, V7X_LANES), lambda i, te, meta: (i, 0)),
        ),
        out_shape=jax.ShapeDtypeStruct(xs.shape, F32),
        compiler_params=pltpu.CompilerParams(
            dimension_semantics=("arbitrary",), vmem_limit_bytes=BIG_VMEM_LIMIT),
        name="moe_experts",
    )(tile_expert, meta, xs, wg, wu, wd)


def _combine_copies(ys_ref, buf_ref, pos_ref, sem, group):
    base = pl.multiple_of(group * V7X_SUBLANES, V7X_SUBLANES)
    copies = []
    for u in range(V7X_SUBLANES):
        for k in range(TOP_K):
            src_row = pos_ref[TOP_K * base + (TOP_K * u + k)]
            copies.append(pltpu.make_async_copy(
                _row_tile(ys_ref, src_row), _row_tile(buf_ref.at[k], base + u), sem))
    return copies


def _combine_kernel(pos_ref, x_ref, gate_ref, ys_ref, o_ref, buf_ref, sem):
    groups = x_ref.shape[0] // V7X_SUBLANES

    def start(g, carry):
        for i, cp in enumerate(_combine_copies(ys_ref, buf_ref, pos_ref, sem, g)):
            cp.start(priority=i % 2)
        return carry

    def wait(g, carry):
        for cp in _combine_copies(ys_ref, buf_ref, pos_ref, sem, g):
            cp.wait()
        return carry

    lax.fori_loop(0, groups, start, 0)
    lax.fori_loop(0, groups, wait, 0)
    gate = gate_ref[...]
    o_ref[...] = x_ref[...] + (gate[:, 0:1] * _load_row_tiles(buf_ref.at[0])
                               + gate[:, 1:2] * _load_row_tiles(buf_ref.at[1]))


def _combine(x2, gate, ys, pos_flat):
    t, d = x2.shape
    tm = min(MOVE_TILE, t)
    return pl.pallas_call(
        _combine_kernel,
        grid=(t // tm,),
        in_specs=[
            pl.BlockSpec((TOP_K * tm,), lambda i: (i,), memory_space=pltpu.SMEM),
            pl.BlockSpec((tm, d), lambda i: (i, 0)),
            pl.BlockSpec((tm, TOP_K), lambda i: (i, 0)),
            pl.BlockSpec(memory_space=pl.ANY),
        ],
        out_specs=pl.BlockSpec((tm, d), lambda i: (i, 0)),
        out_shape=jax.ShapeDtypeStruct((t, d), F32),
        scratch_shapes=[pltpu.VMEM((TOP_K, tm * V7X_SUBLANES, V7X_LANES), F32),
                        pltpu.SemaphoreType.DMA(())],
        compiler_params=pltpu.CompilerParams(dimension_semantics=("arbitrary",)),
        name="moe_combine",
    )(pos_flat, x2, gate, ys)


def _moe(x2, h3, idx, gate, wg, wu, wd, layer):
    t = x2.shape[0]
    tm = GROUP_TILE
    rank, counts = _route_ranks(idx)
    counts = counts[0, :N_EXPERTS]
    padded = (counts + tm - 1) // tm * tm
    ends = jnp.cumsum(padded)
    starts = ends - padded
    pos = starts[idx] + rank
    n_tiles = TOP_K * t // tm + N_EXPERTS
    tile_start = jnp.arange(n_tiles, dtype=jnp.int32) * tm
    tile_expert = jnp.minimum(
        jnp.sum((tile_start[:, None] >= ends[None, :]).astype(jnp.int32), axis=1), N_EXPERTS - 1)
    meta = jnp.concatenate([ends, ends[-1:] // tm]).astype(jnp.int32)
    pos_flat = pos.reshape(-1).astype(jnp.int32)
    xs = _dispatch(h3, pos_flat, meta, n_tiles * tm)
    ys = _experts(xs, tile_expert.astype(jnp.int32), meta, wg, wu, wd, layer)
    return _combine(x2, gate, ys, pos_flat)


def kernel(x, positions, norm_mix, norm_ffn, conv_w_in, conv_w, conv_w_out, mla_w_down, mla_q_a_norm, mla_kv_a_norm, mla_w_uq, mla_w_ukv, mla_q_norm, mla_k_norm, mla_w_o, ffn_w_gate, ffn_w_up, ffn_w_down, moe_router, moe_w_gate, moe_w_up, moe_w_down):
    bsz, seq, d = x.shape
    depth = norm_mix.shape[0]
    cos_t, sinm_t = _rope_tables(positions)
    conv_w_in, conv_w_out, mla_w_o = (w.astype(BF16) for w in (conv_w_in, conv_w_out, mla_w_o))
    ffn_w_gate, ffn_w_up, ffn_w_down = (w.astype(BF16) for w in (ffn_w_gate, ffn_w_up, ffn_w_down))
    moe_w_gate, moe_w_up, moe_w_down = (w.astype(BF16) for w in (moe_w_gate, moe_w_up, moe_w_down))
    for i in range(depth):
        j = i // 2
        if i % 2 == 0:
            x = _conv_mixer(x, norm_mix[i], conv_w_in, conv_w, conv_w_out, j)
            x = _dense_ffn(x.reshape(bsz * seq, d), norm_ffn[i], ffn_w_gate, ffn_w_up, ffn_w_down,
                           j).reshape(bsz, seq, d)
        else:
            q, k, v = _mla_proj(x, norm_mix[i], mla_w_down[j], mla_q_a_norm[j], mla_kv_a_norm[j],
                                mla_w_uq[j], mla_w_ukv[j], mla_q_norm[j], mla_k_norm[j], cos_t, sinm_t)
            o = _attention(q, k, v)
            x2, h3, idx, gate = _oproj_route(
                x.reshape(bsz * seq, d), o.reshape(bsz * seq, -1), mla_w_o, norm_ffn[i], moe_router[j], j)
            x = _moe(x2, h3, idx, gate, moe_w_gate, moe_w_up, moe_w_down, j).reshape(bsz, seq, d)
    return x
```

```python
import jax
import jax.numpy as jnp
from jax import lax
from jax.experimental import pallas as pl
from jax.experimental.pallas import tpu as pltpu

N_HEADS = 8
QK_NOPE_DIM = 128
QK_ROPE_DIM = 64
QK_HEAD_DIM = QK_NOPE_DIM + QK_ROPE_DIM
V_HEAD_DIM = 128
Q_LORA_RANK = 384
KV_LORA_RANK = 256
ROPE_THETA = 10000.0
N_EXPERTS = 8
TOP_K = 2
EPS = 1e-6

V7X_LANES = 128
V7X_SUBLANES = 8
V7X_VMEM_BYTES = 64 * 1024 * 1024
BIG_VMEM_LIMIT = V7X_VMEM_BYTES - 8 * 1024 * 1024

QK_PAD_DIM = 2 * V7X_LANES
Q_HEAD_COLS = 3 * V7X_LANES
MASK_VALUE = -1e30

ROW_TILE = 512
SUB_TILE = 256
ATTN_TILE = 512
GROUP_TILE = 512
MOVE_TILE = 512

F32 = jnp.float32
BF16 = jnp.bfloat16


def _rms(x, gain):
    return x * lax.rsqrt(jnp.mean(x * x, axis=-1, keepdims=True) + EPS) * gain


def _dot(a, b):
    return jnp.dot(a, b, preferred_element_type=F32)


def _resident(shape, layer=None):
    nd = len(shape)
    if layer is None:
        return pl.BlockSpec(shape, lambda *_: (0,) * nd, pipeline_mode=pl.Buffered(1))
    return pl.BlockSpec((None,) + tuple(shape[1:]), lambda *_: (layer,) + (0,) * (nd - 1),
                        pipeline_mode=pl.Buffered(1))


def _conv_kernel(x_ref, g_ref, w_in_ref, cw_ref, w_out_ref, o_ref, carry_ref):
    d = x_ref.shape[-1]
    tm = x_ref.shape[1]

    @pl.when(pl.program_id(1) == 0)
    def _():
        carry_ref[...] = jnp.zeros_like(carry_ref)

    x = x_ref[0]
    h = _rms(x, g_ref[...]).astype(BF16)
    y = _dot(h, w_in_ref[...])
    b_gate = y[:, :d]
    v = y[:, d:2 * d] * y[:, 2 * d:]
    prev = carry_ref[...]
    p1 = prev[V7X_SUBLANES - 1:V7X_SUBLANES]
    p2 = prev[V7X_SUBLANES - 2:V7X_SUBLANES - 1]
    rows = lax.broadcasted_iota(jnp.int32, (tm, d), 0)
    v1 = jnp.where(rows == 0, p1, pltpu.roll(v, 1, 0))
    v2 = jnp.where(rows == 0, p2, jnp.where(rows == 1, p1, pltpu.roll(v, 2, 0)))
    carry_ref[...] = v[tm - V7X_SUBLANES:]
    cw = cw_ref[...]
    conv = v2 * cw[0:1] + v1 * cw[1:2] + v * cw[2:3]
    z = (b_gate * conv).astype(BF16)
    o_ref[0] = x + _dot(z, w_out_ref[...])


def _conv_mixer(x, gain, w_in, conv_w, w_out, layer):
    bsz, seq, d = x.shape
    tm = min(ROW_TILE, seq)
    return pl.pallas_call(
        _conv_kernel,
        grid=(bsz, seq // tm),
        in_specs=[
            pl.BlockSpec((1, tm, d), lambda b, s: (b, s, 0)),
            _resident((1, d)),
            _resident(w_in.shape, layer),
            _resident(conv_w.shape, layer),
            _resident(w_out.shape, layer),
        ],
        out_specs=pl.BlockSpec((1, tm, d), lambda b, s: (b, s, 0)),
        out_shape=jax.ShapeDtypeStruct(x.shape, F32),
        scratch_shapes=[pltpu.VMEM((V7X_SUBLANES, d), F32)],
        compiler_params=pltpu.CompilerParams(
            dimension_semantics=("arbitrary", "arbitrary"), vmem_limit_bytes=BIG_VMEM_LIMIT),
        name="conv_mixer",
    )(x, gain.reshape(1, d), w_in, conv_w, w_out)


def _ffn_kernel(x_ref, g_ref, wg_ref, wu_ref, wd_ref, o_ref):
    x = x_ref[...]
    h = _rms(x, g_ref[...]).astype(BF16)
    gate = _dot(h, wg_ref[...])
    up = _dot(h, wu_ref[...])
    act = (gate * jax.nn.sigmoid(gate) * up).astype(BF16)
    o_ref[...] = x + _dot(act, wd_ref[...])


def _dense_ffn(x2, gain, wg, wu, wd, layer):
    t, d = x2.shape
    tm = min(ROW_TILE, t)
    return pl.pallas_call(
        _ffn_kernel,
        grid=(t // tm,),
        in_specs=[
            pl.BlockSpec((tm, d), lambda i: (i, 0)),
            _resident((1, d)),
            _resident(wg.shape, layer),
            _resident(wu.shape, layer),
            _resident(wd.shape, layer),
        ],
        out_specs=pl.BlockSpec((tm, d), lambda i: (i, 0)),
        out_shape=jax.ShapeDtypeStruct(x2.shape, F32),
        compiler_params=pltpu.CompilerParams(
            dimension_semantics=("arbitrary",), vmem_limit_bytes=BIG_VMEM_LIMIT),
        name="dense_ffn",
    )(x2, gain.reshape(1, d), wg, wu, wd)


def _rope_table_kernel(pos_ref, inv_ref, cos_ref, sin_ref, nsin_ref):
    ang = pos_ref[...] * inv_ref[...]
    s = jnp.sin(ang)
    cos_ref[...] = jnp.cos(ang)
    sin_ref[...] = s
    nsin_ref[...] = -s


def _rope_tables(positions):
    bsz, seq = positions.shape
    half = QK_ROPE_DIM // 2
    t = bsz * seq
    inv_freq = ROPE_THETA ** (-jnp.arange(0, QK_ROPE_DIM, 2, dtype=F32) / QK_ROPE_DIM)
    per_row = V7X_LANES // half
    rows = t // per_row
    pos_rep = jnp.repeat(positions.reshape(-1).astype(F32), half).reshape(rows, V7X_LANES)
    inv_rep = jnp.tile(inv_freq, per_row).reshape(1, V7X_LANES)
    tr = min(1024, rows)
    shp = jax.ShapeDtypeStruct((rows, V7X_LANES), F32)
    cos, sin, nsin = pl.pallas_call(
        _rope_table_kernel,
        grid=(rows // tr,),
        in_specs=[pl.BlockSpec((tr, V7X_LANES), lambda i: (i, 0)),
                  pl.BlockSpec((1, V7X_LANES), lambda i: (0, 0))],
        out_specs=[pl.BlockSpec((tr, V7X_LANES), lambda i: (i, 0))] * 3,
        out_shape=[shp, shp, shp],
        name="rope_tables",
    )(pos_rep, inv_rep)
    cos, sin, nsin = (a.reshape(bsz, seq, half) for a in (cos, sin, nsin))
    zeros = jnp.zeros((bsz, seq, V7X_LANES - QK_ROPE_DIM), F32)
    cos_t = jnp.concatenate([cos, cos, zeros], axis=-1)
    sinm_t = jnp.concatenate([nsin, sin, zeros], axis=-1)
    return cos_t, sinm_t


def _mla_proj_kernel(x_ref, g_ref, wdn_ref, qa_ref, kva_ref, wuq_ref, wukv_ref,
                     qn_ref, kn_ref, cos_ref, sinm_ref, q_ref, k_ref, v_ref):
    qn = qn_ref[...]
    kn = kn_ref[...]
    scale = QK_HEAD_DIM ** -0.5
    tm = x_ref.shape[1]
    sub = min(SUB_TILE, tm)
    lane = lax.broadcasted_iota(jnp.int32, (1, V7X_LANES), 1)
    rope_lanes = lane < QK_ROPE_DIM
    r = lax.broadcasted_iota(jnp.int32, (QK_PAD_DIM, QK_PAD_DIM), 0)
    c = lax.broadcasted_iota(jnp.int32, (QK_PAD_DIM, QK_PAD_DIM), 1)
    pair_ones = jnp.where((r < V7X_LANES) == (c < V7X_LANES), 1.0, 0.0).astype(BF16)

    def pair_sums(a, b):
        sums = _dot(jnp.concatenate([a, b], axis=-1).astype(BF16), pair_ones)
        return sums[:, :V7X_LANES], sums[:, V7X_LANES:]

    for r0 in range(0, tm, sub):
        rows = slice(r0, r0 + sub)
        h = _rms(x_ref[0, rows, :], g_ref[...]).astype(BF16)
        down = _dot(h, wdn_ref[...])
        c_q = _rms(down[:, :Q_LORA_RANK], qa_ref[...]).astype(BF16)
        c_kv = _rms(down[:, Q_LORA_RANK:Q_LORA_RANK + KV_LORA_RANK], kva_ref[...]).astype(BF16)
        k_rope = down[:, Q_LORA_RANK + KV_LORA_RANK:]
        q = _dot(c_q, wuq_ref[...])
        kv = _dot(c_kv, wukv_ref[...])
        cos = cos_ref[0, rows, :]
        sinm = sinm_ref[0, rows, :]
        kr = k_rope * kn[:, V7X_LANES:]
        kr_rot = kr * cos + pltpu.roll(kr, QK_ROPE_DIM, 1) * sinm
        kr_sq = jnp.where(rope_lanes, k_rope * k_rope, 0.0)
        for h0 in range(0, N_HEADS, 2):
            heads = (h0, h0 + 1)
            q3 = [q[:, hd * Q_HEAD_COLS:(hd + 1) * Q_HEAD_COLS] for hd in heads]
            k_nope = [kv[:, hd * QK_PAD_DIM:hd * QK_PAD_DIM + QK_NOPE_DIM] for hd in heads]
            q_ss = pair_sums(*[t[:, :V7X_LANES] * t[:, :V7X_LANES]
                               + t[:, V7X_LANES:QK_PAD_DIM] * t[:, V7X_LANES:QK_PAD_DIM] for t in q3])
            k_ss = pair_sums(*[t * t + kr_sq for t in k_nope])
            for i, hd in enumerate(heads):
                inv_q = lax.rsqrt(q_ss[i] / QK_HEAD_DIM + EPS) * scale
                qh = [q3[i][:, j * V7X_LANES:(j + 1) * V7X_LANES] * inv_q
                      * qn[:, j * V7X_LANES:(j + 1) * V7X_LANES] for j in range(3)]
                q_ref[0, hd, rows, :V7X_LANES] = qh[0].astype(BF16)
                q_ref[0, hd, rows, V7X_LANES:] = (qh[1] * cos + qh[2] * sinm).astype(BF16)
                inv_k = lax.rsqrt(k_ss[i] / QK_HEAD_DIM + EPS)
                k_ref[0, hd, rows, :V7X_LANES] = (k_nope[i] * inv_k * kn[:, :V7X_LANES]).astype(BF16)
                k_ref[0, hd, rows, V7X_LANES:] = (kr_rot * inv_k).astype(BF16)
                v_ref[0, hd, rows, :] = kv[:, hd * QK_PAD_DIM + QK_NOPE_DIM:(hd + 1) * QK_PAD_DIM].astype(BF16)


def _swap_halves(a):
    half = QK_ROPE_DIM // 2
    return jnp.concatenate([a[..., half:], a[..., :half]], axis=-1)


def _mla_proj(x, gain, w_down, q_a_norm, kv_a_norm, w_uq, w_ukv, q_norm, k_norm, cos_t, sinm_t):
    bsz, seq, d = x.shape
    tm = min(ROW_TILE, seq)
    lane_pad = V7X_LANES - QK_ROPE_DIM
    w_rope = w_down[:, Q_LORA_RANK + KV_LORA_RANK:]
    wdn = jnp.concatenate([w_down, _swap_halves(w_rope)], axis=1).astype(BF16)
    wuq = w_uq.reshape(Q_LORA_RANK, N_HEADS, QK_HEAD_DIM)
    wq_rope = wuq[..., QK_NOPE_DIM:]
    zeros = jnp.zeros((Q_LORA_RANK, N_HEADS, lane_pad), w_uq.dtype)
    wuq = jnp.concatenate([wuq[..., :QK_NOPE_DIM], wq_rope, zeros, _swap_halves(wq_rope), zeros], axis=-1)
    wuq = wuq.reshape(Q_LORA_RANK, N_HEADS * Q_HEAD_COLS).astype(BF16)
    wukv = w_ukv.astype(BF16)
    gz = jnp.zeros((lane_pad,), F32)
    qn = jnp.concatenate([q_norm[:QK_NOPE_DIM], q_norm[QK_NOPE_DIM:], gz,
                          _swap_halves(q_norm[QK_NOPE_DIM:]), gz]).reshape(1, Q_HEAD_COLS)
    kn = jnp.concatenate([k_norm, _swap_halves(k_norm[QK_NOPE_DIM:])]).reshape(1, QK_PAD_DIM)
    row_spec = pl.BlockSpec((1, tm, d), lambda b, s: (b, s, 0))
    tab_spec = pl.BlockSpec((1, tm, V7X_LANES), lambda b, s: (b, s, 0))
    qk_spec = pl.BlockSpec((1, N_HEADS, tm, QK_PAD_DIM), lambda b, s: (b, 0, s, 0))
    v_spec = pl.BlockSpec((1, N_HEADS, tm, V_HEAD_DIM), lambda b, s: (b, 0, s, 0))
    qk_shape = jax.ShapeDtypeStruct((bsz, N_HEADS, seq, QK_PAD_DIM), BF16)
    v_shape = jax.ShapeDtypeStruct((bsz, N_HEADS, seq, V_HEAD_DIM), BF16)
    return pl.pallas_call(
        _mla_proj_kernel,
        grid=(bsz, seq // tm),
        in_specs=[
            row_spec,
            _resident((1, d)),
            _resident(wdn.shape),
            _resident((1, Q_LORA_RANK)),
            _resident((1, KV_LORA_RANK)),
            _resident(wuq.shape),
            _resident(wukv.shape),
            _resident(qn.shape),
            _resident(kn.shape),
            tab_spec,
            tab_spec,
        ],
        out_specs=[qk_spec, qk_spec, v_spec],
        out_shape=[qk_shape, qk_shape, v_shape],
        compiler_params=pltpu.CompilerParams(
            dimension_semantics=("arbitrary", "arbitrary"), vmem_limit_bytes=BIG_VMEM_LIMIT),
        name="mla_proj",
    )(x, gain.reshape(1, d), wdn, q_a_norm.reshape(1, -1), kv_a_norm.reshape(1, -1), wuq, wukv,
      qn, kn, cos_t, sinm_t)


def _attn_kernel(q_ref, k_ref, v_ref, o_ref):
    seq = q_ref.shape[2]
    dv = v_ref.shape[3]
    t = min(ATTN_TILE, seq)
    dims = (((1,), (1,)), ((), ()))
    row = lax.broadcasted_iota(jnp.int32, (t, t), 0)
    col = lax.broadcasted_iota(jnp.int32, (t, t), 1)
    for qi in range(seq // t):
        q = q_ref[0, 0, qi * t:(qi + 1) * t, :]
        m = acc = None
        for ki in range(qi + 1):
            k = k_ref[0, 0, ki * t:(ki + 1) * t, :]
            v = v_ref[0, 0, ki * t:(ki + 1) * t, :]
            v_ext = jnp.concatenate([v, jnp.ones_like(v)], axis=-1)
            s = lax.dot_general(q, k, dims, preferred_element_type=F32)
            if ki == qi:
                s = jnp.where(col <= row, s, MASK_VALUE)
            s_max = jnp.max(s, axis=-1, keepdims=True)
            if m is None:
                m = s_max
                acc = _dot(jnp.exp(s - m).astype(BF16), v_ext)
            else:
                m_new = jnp.maximum(m, s_max)
                acc = jnp.exp(m - m_new) * acc + _dot(jnp.exp(s - m_new).astype(BF16), v_ext)
                m = m_new
        o_ref[0, qi * t:(qi + 1) * t, :] = (acc[:, :dv] / acc[:, dv:]).astype(o_ref.dtype)


def _attention(q, k, v):
    bsz, nh, seq, _ = q.shape
    dv = v.shape[-1]
    head_spec = lambda last: pl.BlockSpec((1, 1, seq, last), lambda b, h: (b, h, 0, 0))
    return pl.pallas_call(
        _attn_kernel,
        grid=(bsz, nh),
        in_specs=[head_spec(q.shape[-1]), head_spec(k.shape[-1]), head_spec(dv)],
        out_specs=pl.BlockSpec((1, seq, dv), lambda b, h: (b, 0, h)),
        out_shape=jax.ShapeDtypeStruct((bsz, seq, nh * dv), BF16),
        compiler_params=pltpu.CompilerParams(dimension_semantics=("arbitrary", "arbitrary")),
        name="causal_attention",
    )(q, k, v)


def _oproj_route_kernel(x_ref, o_ref, wo_ref, g_ref, wr_ref, xo_ref, h_ref, idx_ref, gate_ref):
    x = x_ref[...] + _dot(o_ref[...], wo_ref[...])
    xo_ref[...] = x
    hn = _rms(x, g_ref[...])
    _store_row_tiles(h_ref, hn)
    logits = lax.dot_general(wr_ref[...], hn.astype(BF16), (((1,), (1,)), ((), ())),
                             preferred_element_type=F32)[:N_EXPERTS]
    expert = lax.broadcasted_iota(jnp.int32, logits.shape, 0)
    e = jnp.exp(logits - jnp.max(logits, axis=0, keepdims=True))
    probs = e / jnp.sum(e, axis=0, keepdims=True)
    p1 = jnp.max(probs, axis=0, keepdims=True)
    i1 = jnp.min(jnp.where(probs == p1, expert, N_EXPERTS), axis=0, keepdims=True)
    rest = jnp.where(expert == i1, -1.0, probs)
    p2 = jnp.max(rest, axis=0, keepdims=True)
    i2 = jnp.min(jnp.where(rest == p2, expert, N_EXPERTS), axis=0, keepdims=True)
    denom = p1 + p2
    idx_ref[...] = jnp.concatenate([i1, i2], axis=0)
    gate_ref[...] = jnp.concatenate([p1 / denom, p2 / denom], axis=0)


def _oproj_route(x2, o2, w_o, gain, router, layer):
    t, d = x2.shape
    assert d == V7X_SUBLANES * V7X_LANES, "row-DMA layout stores one token row per (8, 128) tile"
    tm = min(ROW_TILE, t)
    bf16_rows = 2 * V7X_SUBLANES
    wr = jnp.pad(router.T, ((0, bf16_rows - N_EXPERTS), (0, 0))).astype(BF16)
    return pl.pallas_call(
        _oproj_route_kernel,
        grid=(t // tm,),
        in_specs=[
            pl.BlockSpec((tm, d), lambda i: (i, 0)),
            pl.BlockSpec((tm, o2.shape[1]), lambda i: (i, 0)),
            _resident(w_o.shape, layer),
            _resident((1, d)),
            _resident(wr.shape),
        ],
        out_specs=[
            pl.BlockSpec((tm, d), lambda i: (i, 0)),
            pl.BlockSpec((tm * V7X_SUBLANES, V7X_LANES), lambda i: (i, 0)),
            pl.BlockSpec((TOP_K, tm), lambda i: (0, i)),
            pl.BlockSpec((TOP_K, tm), lambda i: (0, i)),
        ],
        out_shape=[
            jax.ShapeDtypeStruct((t, d), F32),
            jax.ShapeDtypeStruct((t * V7X_SUBLANES, V7X_LANES), F32),
            jax.ShapeDtypeStruct((TOP_K, t), jnp.int32),
            jax.ShapeDtypeStruct((TOP_K, t), F32),
        ],
        compiler_params=pltpu.CompilerParams(
            dimension_semantics=("arbitrary",), vmem_limit_bytes=BIG_VMEM_LIMIT),
        name="oproj_route",
    )(x2, o2, w_o, gain.reshape(1, d), wr)


def _rank_kernel(idx_ref, rank_ref, count_ref, carry_ref):
    tm = idx_ref.shape[0]

    @pl.when(pl.program_id(0) == 0)
    def _():
        carry_ref[...] = jnp.zeros_like(carry_ref)

    idx = idx_ref[...]
    i1 = idx[:, 0:1]
    i2 = idx[:, 1:2]
    lane = lax.broadcasted_iota(jnp.int32, (tm, V7X_LANES), 1)
    onehot = jnp.where((lane == i1) | (lane == i2), 1.0, 0.0)
    r = lax.broadcasted_iota(jnp.int32, (tm, tm), 0)
    c = lax.broadcasted_iota(jnp.int32, (tm, tm), 1)
    lower = jnp.where(c < r, 1.0, 0.0).astype(BF16)
    before = _dot(lower, onehot.astype(BF16)) + carry_ref[...]
    r1 = jnp.sum(jnp.where(lane == i1, before, 0.0), axis=-1, keepdims=True)
    r2 = jnp.sum(jnp.where(lane == i2, before, 0.0), axis=-1, keepdims=True)
    slot = lax.broadcasted_iota(jnp.int32, (tm, TOP_K), 1)
    rank_ref[...] = jnp.where(slot == 0, r1, r2).astype(jnp.int32)
    total = carry_ref[...] + jnp.sum(onehot, axis=0, keepdims=True)
    carry_ref[...] = total
    count_ref[...] = total.astype(jnp.int32)


def _route_ranks(idx):
    t = idx.shape[0]
    tm = min(ROW_TILE, t)
    return pl.pallas_call(
        _rank_kernel,
        grid=(t // tm,),
        in_specs=[pl.BlockSpec((tm, TOP_K), lambda i: (i, 0))],
        out_specs=[pl.BlockSpec((tm, TOP_K), lambda i: (i, 0)),
                   pl.BlockSpec((1, V7X_LANES), lambda i: (0, 0))],
        out_shape=[jax.ShapeDtypeStruct((t, TOP_K), jnp.int32),
                   jax.ShapeDtypeStruct((1, V7X_LANES), jnp.int32)],
        scratch_shapes=[pltpu.VMEM((1, V7X_LANES), F32)],
        compiler_params=pltpu.CompilerParams(dimension_semantics=("arbitrary",)),
        name="route_ranks",
    )(idx)


def _load_row_tiles(tiles_ref):
    rows = tiles_ref.shape[0] // V7X_SUBLANES
    return jnp.concatenate(
        [tiles_ref[pl.ds(c, rows, stride=V7X_SUBLANES), :] for c in range(V7X_SUBLANES)], axis=-1)


def _store_row_tiles(tiles_ref, value):
    rows = value.shape[0]
    for c in range(V7X_SUBLANES):
        tiles_ref[pl.ds(c, rows, stride=V7X_SUBLANES), :] = value[:, c * V7X_LANES:(c + 1) * V7X_LANES]


def _row_tile(ref, row):
    return ref.at[pl.ds(pl.multiple_of(row * V7X_SUBLANES, V7X_SUBLANES), V7X_SUBLANES), :]


def _dispatch_copies(h_ref, xs_ref, pos_ref, sem, group):
    base = pl.multiple_of(group * V7X_SUBLANES, V7X_SUBLANES)
    copies = []
    for u in range(V7X_SUBLANES):
        for k in range(TOP_K):
            dst_row = pos_ref[TOP_K * base + (TOP_K * u + k)]
            copies.append(pltpu.make_async_copy(
                _row_tile(h_ref, base + u), _row_tile(xs_ref, dst_row), sem))
    return copies


def _pad_tile_copies(meta_ref, zero_ref, xs_ref, sem):
    tile_sublanes = zero_ref.shape[0]
    tm = tile_sublanes // V7X_SUBLANES
    n_tiles = xs_ref.shape[0] // tile_sublanes

    def zero_tile(first_row):
        first = pl.multiple_of(first_row * V7X_SUBLANES, tile_sublanes)
        return pltpu.make_async_copy(zero_ref, xs_ref.at[pl.ds(first, tile_sublanes), :], sem)

    pairs = []
    for e in range(N_EXPERTS):
        end = meta_ref[e]
        pairs.append((end >= tm, zero_tile(jnp.maximum(end - tm, 0))))
    n_active = meta_ref[N_EXPERTS]
    for j in range(N_EXPERTS):
        tile = n_tiles - 1 - j
        pairs.append((tile >= n_active, zero_tile(tile * tm)))
    return pairs


def _dispatch_kernel(meta_ref, pos_ref, h_ref, xs_ref, zero_ref, zero_sem, sem):
    groups = h_ref.shape[0] // (V7X_SUBLANES * V7X_SUBLANES)

    @pl.when(pl.program_id(0) == 0)
    def _():
        zero_ref[...] = jnp.zeros_like(zero_ref)
        pairs = _pad_tile_copies(meta_ref, zero_ref, xs_ref, zero_sem)
        for cond, cp in pairs:
            pl.when(cond)(cp.start)
        for cond, cp in pairs:
            pl.when(cond)(cp.wait)

    def start(g, carry):
        for i, cp in enumerate(_dispatch_copies(h_ref, xs_ref, pos_ref, sem, g)):
            cp.start(priority=i % 2)
        return carry

    def wait(g, carry):
        for cp in _dispatch_copies(h_ref, xs_ref, pos_ref, sem, g):
            cp.wait()
        return carry

    lax.fori_loop(0, groups, start, 0)
    lax.fori_loop(0, groups, wait, 0)


def _dispatch(h3, pos_flat, meta, n_rows):
    t = h3.shape[0] // V7X_SUBLANES
    tm = min(MOVE_TILE, t)
    return pl.pallas_call(
        _dispatch_kernel,
        grid_spec=pltpu.PrefetchScalarGridSpec(
            num_scalar_prefetch=1,
            grid=(t // tm,),
            in_specs=[
                pl.BlockSpec((TOP_K * tm,), lambda i, meta: (i,), memory_space=pltpu.SMEM),
                pl.BlockSpec((tm * V7X_SUBLANES, V7X_LANES), lambda i, meta: (i, 0)),
            ],
            out_specs=pl.BlockSpec(memory_space=pl.ANY),
            scratch_shapes=[pltpu.VMEM((GROUP_TILE * V7X_SUBLANES, V7X_LANES), h3.dtype),
                            pltpu.SemaphoreType.DMA(()), pltpu.SemaphoreType.DMA(())],
        ),
        out_shape=jax.ShapeDtypeStruct((n_rows * V7X_SUBLANES, V7X_LANES), h3.dtype),
        compiler_params=pltpu.CompilerParams(dimension_semantics=("arbitrary",)),
        name="moe_dispatch",
    )(meta, pos_flat, h3)


def _experts_kernel(te_ref, meta_ref, x_ref, wg_ref, wu_ref, wd_ref, y_ref):
    del te_ref
    n_active = meta_ref[N_EXPERTS]

    @pl.when(pl.program_id(0) < n_active)
    def _():
        h = _load_row_tiles(x_ref).astype(BF16)
        gate = _dot(h, wg_ref[...])
        up = _dot(h, wu_ref[...])
        act = (gate * jax.nn.sigmoid(gate) * up).astype(BF16)
        _store_row_tiles(y_ref, _dot(act, wd_ref[...]))

    @pl.when(pl.program_id(0) >= n_active)
    def _():
        y_ref[...] = jnp.zeros_like(y_ref)


def _experts(xs, tile_expert, meta, wg, wu, wd, layer):
    tile_sublanes = GROUP_TILE * V7X_SUBLANES
    d, f = wg.shape[-2:]

    def row_map(i, te, meta):
        return (jnp.minimum(i, meta[N_EXPERTS] - 1), 0)

    def w_map(i, te, meta):
        return (layer, te[jnp.minimum(i, meta[N_EXPERTS] - 1)], 0, 0)

    return pl.pallas_call(
        _experts_kernel,
        grid_spec=pltpu.PrefetchScalarGridSpec(
            num_scalar_prefetch=2,
            grid=(xs.shape[0] // tile_sublanes,),
            in_specs=[
                pl.BlockSpec((tile_sublanes, V7X_LANES), row_map),
                pl.BlockSpec((None, None, d, f), w_map),
                pl.BlockSpec((None, None, d, f), w_map),
                pl.BlockSpec((None, None, f, d), w_map),
            ],
            out_specs=pl.BlockSpec((tile_sublanes, V7X_LANES), lambda i, te, meta: (i, 0)),
        ),
        out_shape=jax.ShapeDtypeStruct(xs.shape, F32),
        compiler_params=pltpu.CompilerParams(
            dimension_semantics=("arbitrary",), vmem_limit_bytes=BIG_VMEM_LIMIT),
        name="moe_experts",
    )(tile_expert, meta, xs, wg, wu, wd)


def _combine_copies(ys_ref, buf_ref, pos_ref, sem, group):
    base = pl.multiple_of(group * V7X_SUBLANES, V7X_SUBLANES)
    copies = []
    for u in range(V7X_SUBLANES):
        for k in range(TOP_K):
            src_row = pos_ref[TOP_K * base + (TOP_K * u + k)]
            copies.append(pltpu.make_async_copy(
                _row_tile(ys_ref, src_row), _row_tile(buf_ref.at[k], base + u), sem))
    return copies


def _combine_kernel(pos_ref, x_ref, gate_ref, ys_ref, o_ref, buf_ref, sem):
    groups = x_ref.shape[0] // V7X_SUBLANES

    def start(g, carry):
        for i, cp in enumerate(_combine_copies(ys_ref, buf_ref, pos_ref, sem, g)):
            cp.start(priority=i % 2)
        return carry

    def wait(g, carry):
        for cp in _combine_copies(ys_ref, buf_ref, pos_ref, sem, g):
            cp.wait()
        return carry

    lax.fori_loop(0, groups, start, 0)
    lax.fori_loop(0, groups, wait, 0)
    gate = gate_ref[...]
    o_ref[...] = x_ref[...] + (gate[:, 0:1] * _load_row_tiles(buf_ref.at[0])
                               + gate[:, 1:2] * _load_row_tiles(buf_ref.at[1]))


def _combine(x2, gate, ys, pos_flat):
    t, d = x2.shape
    tm = min(MOVE_TILE, t)
    return pl.pallas_call(
        _combine_kernel,
        grid=(t // tm,),
        in_specs=[
            pl.BlockSpec((TOP_K * tm,), lambda i: (i,), memory_space=pltpu.SMEM),
            pl.BlockSpec((tm, d), lambda i: (i, 0)),
            pl.BlockSpec((tm, TOP_K), lambda i: (i, 0)),
            pl.BlockSpec(memory_space=pl.ANY),
        ],
        out_specs=pl.BlockSpec((tm, d), lambda i: (i, 0)),
        out_shape=jax.ShapeDtypeStruct((t, d), F32),
        scratch_shapes=[pltpu.VMEM((TOP_K, tm * V7X_SUBLANES, V7X_LANES), F32),
                        pltpu.SemaphoreType.DMA(())],
        compiler_params=pltpu.CompilerParams(dimension_semantics=("arbitrary",)),
        name="moe_combine",
    )(pos_flat, x2, gate, ys)


def _moe(x2, h3, idx, gate, wg, wu, wd, layer):
    t = x2.shape[0]
    tm = GROUP_TILE
    rank, counts = _route_ranks(idx)
    counts = counts[0, :N_EXPERTS]
    padded = (counts + tm - 1) // tm * tm
    ends = jnp.cumsum(padded)
    starts = ends - padded
    pos = starts[idx] + rank
    n_tiles = TOP_K * t // tm + N_EXPERTS
    tile_start = jnp.arange(n_tiles, dtype=jnp.int32) * tm
    tile_expert = jnp.minimum(
        jnp.sum((tile_start[:, None] >= ends[None, :]).astype(jnp.int32), axis=1), N_EXPERTS - 1)
    meta = jnp.concatenate([ends, ends[-1:] // tm]).astype(jnp.int32)
    pos_flat = pos.reshape(-1).astype(jnp.int32)
    xs = _dispatch(h3, pos_flat, meta, n_tiles * tm)
    ys = _experts(xs, tile_expert.astype(jnp.int32), meta, wg, wu, wd, layer)
    return _combine(x2, gate, ys, pos_flat)


def kernel(x, positions, norm_mix, norm_ffn, conv_w_in, conv_w, conv_w_out, mla_w_down, mla_q_a_norm, mla_kv_a_norm, mla_w_uq, mla_w_ukv, mla_q_norm, mla_k_norm, mla_w_o, ffn_w_gate, ffn_w_up, ffn_w_down, moe_router, moe_w_gate, moe_w_up, moe_w_down):
    bsz, seq, d = x.shape
    depth = norm_mix.shape[0]
    cos_t, sinm_t = _rope_tables(positions)
    conv_w_in, conv_w_out, mla_w_o = (w.astype(BF16) for w in (conv_w_in, conv_w_out, mla_w_o))
    ffn_w_gate, ffn_w_up, ffn_w_down = (w.astype(BF16) for w in (ffn_w_gate, ffn_w_up, ffn_w_down))
    moe_w_gate, moe_w_up, moe_w_down = (w.astype(BF16) for w in (moe_w_gate, moe_w_up, moe_w_down))
    for i in range(depth):
        j = i // 2
        if i % 2 == 0:
            x = _conv_mixer(x, norm_mix[i], conv_w_in, conv_w, conv_w_out, j)
            x = _dense_ffn(x.reshape(bsz * seq, d), norm_ffn[i], ffn_w_gate, ffn_w_up, ffn_w_down,
                           j).reshape(bsz, seq, d)
        else:
            q, k, v = _mla_proj(x, norm_mix[i], mla_w_down[j], mla_q_a_norm[j], mla_kv_a_norm[j],
                                mla_w_uq[j], mla_w_ukv[j], mla_q_norm[j], mla_k_norm[j], cos_t, sinm_t)
            o = _attention(q, k, v)
            x2, h3, idx, gate = _oproj_route(
                x.reshape(bsz * seq, d), o.reshape(bsz * seq, -1), mla_w_o, norm_ffn[i], moe_router[j], j)
            idx, gate = idx.T, gate.T
            x = _moe(x2, h3, idx, gate, moe_w_gate, moe_w_up, moe_w_down, j).reshape(bsz, seq, d)
    return x
```

```python
import jax
import jax.numpy as jnp
from jax import lax
from jax.experimental import pallas as pl
from jax.experimental.pallas import tpu as pltpu

N_HEADS = 8
QK_NOPE_DIM = 128
QK_ROPE_DIM = 64
QK_HEAD_DIM = QK_NOPE_DIM + QK_ROPE_DIM
V_HEAD_DIM = 128
Q_LORA_RANK = 384
KV_LORA_RANK = 256
ROPE_THETA = 10000.0
N_EXPERTS = 8
TOP_K = 2
EPS = 1e-6

V7X_LANES = 128
V7X_SUBLANES = 8
V7X_VMEM_BYTES = 64 * 1024 * 1024
BIG_VMEM_LIMIT = V7X_VMEM_BYTES - 8 * 1024 * 1024

QK_PAD_DIM = 2 * V7X_LANES
Q_HEAD_COLS = 3 * V7X_LANES
MASK_VALUE = -1e30

ROW_TILE = 512
SUB_TILE = 256
ATTN_TILE = 512
GROUP_TILE = 512
MOVE_TILE = 512

F32 = jnp.float32
BF16 = jnp.bfloat16


def _rms(x, gain):
    return x * lax.rsqrt(jnp.mean(x * x, axis=-1, keepdims=True) + EPS) * gain


def _dot(a, b):
    return jnp.dot(a, b, preferred_element_type=F32)


def _resident(shape, layer=None):
    nd = len(shape)
    if layer is None:
        return pl.BlockSpec(shape, lambda *_: (0,) * nd, pipeline_mode=pl.Buffered(1))
    return pl.BlockSpec((None,) + tuple(shape[1:]), lambda *_: (layer,) + (0,) * (nd - 1),
                        pipeline_mode=pl.Buffered(1))


def _conv_kernel(x_ref, g_ref, w_in_ref, cw_ref, w_out_ref, o_ref, carry_ref):
    d = x_ref.shape[-1]
    tm = x_ref.shape[1]

    @pl.when(pl.program_id(1) == 0)
    def _():
        carry_ref[...] = jnp.zeros_like(carry_ref)

    x = x_ref[0]
    h = _rms(x, g_ref[...]).astype(BF16)
    y = _dot(h, w_in_ref[...])
    b_gate = y[:, :d]
    v = y[:, d:2 * d] * y[:, 2 * d:]
    prev = carry_ref[...]
    p1 = prev[V7X_SUBLANES - 1:V7X_SUBLANES]
    p2 = prev[V7X_SUBLANES - 2:V7X_SUBLANES - 1]
    rows = lax.broadcasted_iota(jnp.int32, (tm, d), 0)
    v1 = jnp.where(rows == 0, p1, pltpu.roll(v, 1, 0))
    v2 = jnp.where(rows == 0, p2, jnp.where(rows == 1, p1, pltpu.roll(v, 2, 0)))
    carry_ref[...] = v[tm - V7X_SUBLANES:]
    cw = cw_ref[...]
    conv = v2 * cw[0:1] + v1 * cw[1:2] + v * cw[2:3]
    z = (b_gate * conv).astype(BF16)
    o_ref[0] = x + _dot(z, w_out_ref[...])


def _conv_mixer(x, gain, w_in, conv_w, w_out, layer):
    bsz, seq, d = x.shape
    tm = min(ROW_TILE, seq)
    return pl.pallas_call(
        _conv_kernel,
        grid=(bsz, seq // tm),
        in_specs=[
            pl.BlockSpec((1, tm, d), lambda b, s: (b, s, 0)),
            _resident((1, d)),
            _resident(w_in.shape, layer),
            _resident(conv_w.shape, layer),
            _resident(w_out.shape, layer),
        ],
        out_specs=pl.BlockSpec((1, tm, d), lambda b, s: (b, s, 0)),
        out_shape=jax.ShapeDtypeStruct(x.shape, F32),
        scratch_shapes=[pltpu.VMEM((V7X_SUBLANES, d), F32)],
        compiler_params=pltpu.CompilerParams(
            dimension_semantics=("arbitrary", "arbitrary"), vmem_limit_bytes=BIG_VMEM_LIMIT),
        name="conv_mixer",
    )(x, gain.reshape(1, d), w_in, conv_w, w_out)


def _ffn_kernel(x_ref, g_ref, wg_ref, wu_ref, wd_ref, o_ref):
    x = x_ref[...]
    h = _rms(x, g_ref[...]).astype(BF16)
    gate = _dot(h, wg_ref[...])
    up = _dot(h, wu_ref[...])
    act = (gate * jax.nn.sigmoid(gate) * up).astype(BF16)
    o_ref[...] = x + _dot(act, wd_ref[...])


def _dense_ffn(x2, gain, wg, wu, wd, layer):
    t, d = x2.shape
    tm = min(ROW_TILE, t)
    return pl.pallas_call(
        _ffn_kernel,
        grid=(t // tm,),
        in_specs=[
            pl.BlockSpec((tm, d), lambda i: (i, 0)),
            _resident((1, d)),
            _resident(wg.shape, layer),
            _resident(wu.shape, layer),
            _resident(wd.shape, layer),
        ],
        out_specs=pl.BlockSpec((tm, d), lambda i: (i, 0)),
        out_shape=jax.ShapeDtypeStruct(x2.shape, F32),
        compiler_params=pltpu.CompilerParams(
            dimension_semantics=("arbitrary",), vmem_limit_bytes=BIG_VMEM_LIMIT),
        name="dense_ffn",
    )(x2, gain.reshape(1, d), wg, wu, wd)


def _rope_table_kernel(pos_ref, inv_ref, cos_ref, sin_ref, nsin_ref):
    ang = pos_ref[...] * inv_ref[...]
    s = jnp.sin(ang)
    cos_ref[...] = jnp.cos(ang)
    sin_ref[...] = s
    nsin_ref[...] = -s


def _rope_tables(positions):
    bsz, seq = positions.shape
    half = QK_ROPE_DIM // 2
    t = bsz * seq
    inv_freq = ROPE_THETA ** (-jnp.arange(0, QK_ROPE_DIM, 2, dtype=F32) / QK_ROPE_DIM)
    per_row = V7X_LANES // half
    rows = t // per_row
    pos_rep = jnp.repeat(positions.reshape(-1).astype(F32), half).reshape(rows, V7X_LANES)
    inv_rep = jnp.tile(inv_freq, per_row).reshape(1, V7X_LANES)
    tr = min(1024, rows)
    shp = jax.ShapeDtypeStruct((rows, V7X_LANES), F32)
    cos, sin, nsin = pl.pallas_call(
        _rope_table_kernel,
        grid=(rows // tr,),
        in_specs=[pl.BlockSpec((tr, V7X_LANES), lambda i: (i, 0)),
                  pl.BlockSpec((1, V7X_LANES), lambda i: (0, 0))],
        out_specs=[pl.BlockSpec((tr, V7X_LANES), lambda i: (i, 0))] * 3,
        out_shape=[shp, shp, shp],
        name="rope_tables",
    )(pos_rep, inv_rep)
    cos, sin, nsin = (a.reshape(bsz, seq, half) for a in (cos, sin, nsin))
    zeros = jnp.zeros((bsz, seq, V7X_LANES - QK_ROPE_DIM), F32)
    cos_t = jnp.concatenate([cos, cos, zeros], axis=-1)
    sinm_t = jnp.concatenate([nsin, sin, zeros], axis=-1)
    return cos_t, sinm_t


def _mla_proj_kernel(x_ref, g_ref, wdn_ref, qa_ref, kva_ref, wuq_ref, wukv_ref,
                     qn_ref, kn_ref, cos_ref, sinm_ref, q_ref, k_ref, v_ref):
    qn = qn_ref[...]
    kn = kn_ref[...]
    scale = QK_HEAD_DIM ** -0.5
    tm = x_ref.shape[1]
    sub = min(SUB_TILE, tm)
    lane = lax.broadcasted_iota(jnp.int32, (1, V7X_LANES), 1)
    rope_lanes = lane < QK_ROPE_DIM
    r = lax.broadcasted_iota(jnp.int32, (QK_PAD_DIM, QK_PAD_DIM), 0)
    c = lax.broadcasted_iota(jnp.int32, (QK_PAD_DIM, QK_PAD_DIM), 1)
    pair_ones = jnp.where((r < V7X_LANES) == (c < V7X_LANES), 1.0, 0.0).astype(BF16)

    def pair_sums(a, b):
        sums = _dot(jnp.concatenate([a, b], axis=-1).astype(BF16), pair_ones)
        return sums[:, :V7X_LANES], sums[:, V7X_LANES:]

    for r0 in range(0, tm, sub):
        rows = slice(r0, r0 + sub)
        h = _rms(x_ref[0, rows, :], g_ref[...]).astype(BF16)
        down = _dot(h, wdn_ref[...])
        c_q = _rms(down[:, :Q_LORA_RANK], qa_ref[...]).astype(BF16)
        c_kv = _rms(down[:, Q_LORA_RANK:Q_LORA_RANK + KV_LORA_RANK], kva_ref[...]).astype(BF16)
        k_rope = down[:, Q_LORA_RANK + KV_LORA_RANK:]
        q = _dot(c_q, wuq_ref[...])
        kv = _dot(c_kv, wukv_ref[...])
        cos = cos_ref[0, rows, :]
        sinm = sinm_ref[0, rows, :]
        kr = k_rope * kn[:, V7X_LANES:]
        kr_rot = kr * cos + pltpu.roll(kr, QK_ROPE_DIM, 1) * sinm
        kr_sq = jnp.where(rope_lanes, k_rope * k_rope, 0.0)
        for h0 in range(0, N_HEADS, 2):
            heads = (h0, h0 + 1)
            q3 = [q[:, hd * Q_HEAD_COLS:(hd + 1) * Q_HEAD_COLS] for hd in heads]
            k_nope = [kv[:, hd * QK_PAD_DIM:hd * QK_PAD_DIM + QK_NOPE_DIM] for hd in heads]
            q_ss = pair_sums(*[t[:, :V7X_LANES] * t[:, :V7X_LANES]
                               + t[:, V7X_LANES:QK_PAD_DIM] * t[:, V7X_LANES:QK_PAD_DIM] for t in q3])
            k_ss = pair_sums(*[t * t + kr_sq for t in k_nope])
            for i, hd in enumerate(heads):
                inv_q = lax.rsqrt(q_ss[i] / QK_HEAD_DIM + EPS) * scale
                qh = [q3[i][:, j * V7X_LANES:(j + 1) * V7X_LANES] * inv_q
                      * qn[:, j * V7X_LANES:(j + 1) * V7X_LANES] for j in range(3)]
                q_ref[0, hd, rows, :V7X_LANES] = qh[0].astype(BF16)
                q_ref[0, hd, rows, V7X_LANES:] = (qh[1] * cos + qh[2] * sinm).astype(BF16)
                inv_k = lax.rsqrt(k_ss[i] / QK_HEAD_DIM + EPS)
                k_ref[0, hd, rows, :V7X_LANES] = (k_nope[i] * inv_k * kn[:, :V7X_LANES]).astype(BF16)
                k_ref[0, hd, rows, V7X_LANES:] = (kr_rot * inv_k).astype(BF16)
                v_ref[0, hd, rows, :] = kv[:, hd * QK_PAD_DIM + QK_NOPE_DIM:(hd + 1) * QK_PAD_DIM].astype(BF16)


def _swap_halves(a):
    half = QK_ROPE_DIM // 2
    return jnp.concatenate([a[..., half:], a[..., :half]], axis=-1)


def _mla_proj(x, gain, w_down, q_a_norm, kv_a_norm, w_uq, w_ukv, q_norm, k_norm, cos_t, sinm_t):
    bsz, seq, d = x.shape
    tm = min(ROW_TILE, seq)
    lane_pad = V7X_LANES - QK_ROPE_DIM
    w_rope = w_down[:, Q_LORA_RANK + KV_LORA_RANK:]
    wdn = jnp.concatenate([w_down, _swap_halves(w_rope)], axis=1).astype(BF16)
    wuq = w_uq.reshape(Q_LORA_RANK, N_HEADS, QK_HEAD_DIM)
    wq_rope = wuq[..., QK_NOPE_DIM:]
    zeros = jnp.zeros((Q_LORA_RANK, N_HEADS, lane_pad), w_uq.dtype)
    wuq = jnp.concatenate([wuq[..., :QK_NOPE_DIM], wq_rope, zeros, _swap_halves(wq_rope), zeros], axis=-1)
    wuq = wuq.reshape(Q_LORA_RANK, N_HEADS * Q_HEAD_COLS).astype(BF16)
    wukv = w_ukv.astype(BF16)
    gz = jnp.zeros((lane_pad,), F32)
    qn = jnp.concatenate([q_norm[:QK_NOPE_DIM], q_norm[QK_NOPE_DIM:], gz,
                          _swap_halves(q_norm[QK_NOPE_DIM:]), gz]).reshape(1, Q_HEAD_COLS)
    kn = jnp.concatenate([k_norm, _swap_halves(k_norm[QK_NOPE_DIM:])]).reshape(1, QK_PAD_DIM)
    row_spec = pl.BlockSpec((1, tm, d), lambda b, s: (b, s, 0))
    tab_spec = pl.BlockSpec((1, tm, V7X_LANES), lambda b, s: (b, s, 0))
    qk_spec = pl.BlockSpec((1, N_HEADS, tm, QK_PAD_DIM), lambda b, s: (b, 0, s, 0))
    v_spec = pl.BlockSpec((1, N_HEADS, tm, V_HEAD_DIM), lambda b, s: (b, 0, s, 0))
    qk_shape = jax.ShapeDtypeStruct((bsz, N_HEADS, seq, QK_PAD_DIM), BF16)
    v_shape = jax.ShapeDtypeStruct((bsz, N_HEADS, seq, V_HEAD_DIM), BF16)
    return pl.pallas_call(
        _mla_proj_kernel,
        grid=(bsz, seq // tm),
        in_specs=[
            row_spec,
            _resident((1, d)),
            _resident(wdn.shape),
            _resident((1, Q_LORA_RANK)),
            _resident((1, KV_LORA_RANK)),
            _resident(wuq.shape),
            _resident(wukv.shape),
            _resident(qn.shape),
            _resident(kn.shape),
            tab_spec,
            tab_spec,
        ],
        out_specs=[qk_spec, qk_spec, v_spec],
        out_shape=[qk_shape, qk_shape, v_shape],
        compiler_params=pltpu.CompilerParams(
            dimension_semantics=("arbitrary", "arbitrary"), vmem_limit_bytes=BIG_VMEM_LIMIT),
        name="mla_proj",
    )(x, gain.reshape(1, d), wdn, q_a_norm.reshape(1, -1), kv_a_norm.reshape(1, -1), wuq, wukv,
      qn, kn, cos_t, sinm_t)


def _attn_kernel(q_ref, k_ref, v_ref, o_ref):
    seq = q_ref.shape[2]
    dv = v_ref.shape[3]
    t = min(ATTN_TILE, seq)
    dims = (((1,), (1,)), ((), ()))
    row = lax.broadcasted_iota(jnp.int32, (t, t), 0)
    col = lax.broadcasted_iota(jnp.int32, (t, t), 1)
    for qi in range(seq // t):
        q = q_ref[0, 0, qi * t:(qi + 1) * t, :]
        m = acc = None
        for ki in range(qi + 1):
            k = k_ref[0, 0, ki * t:(ki + 1) * t, :]
            v = v_ref[0, 0, ki * t:(ki + 1) * t, :]
            v_ext = jnp.concatenate([v, jnp.ones_like(v)], axis=-1)
            s = lax.dot_general(q, k, dims, preferred_element_type=F32)
            if ki == qi:
                s = jnp.where(col <= row, s, MASK_VALUE)
            s_max = jnp.max(s, axis=-1, keepdims=True)
            if m is None:
                m = s_max
                acc = _dot(jnp.exp(s - m).astype(BF16), v_ext)
            else:
                m_new = jnp.maximum(m, s_max)
                acc = jnp.exp(m - m_new) * acc + _dot(jnp.exp(s - m_new).astype(BF16), v_ext)
                m = m_new
        o_ref[0, qi * t:(qi + 1) * t, :] = (acc[:, :dv] / acc[:, dv:]).astype(o_ref.dtype)


def _attention(q, k, v):
    bsz, nh, seq, _ = q.shape
    dv = v.shape[-1]
    head_spec = lambda last: pl.BlockSpec((1, 1, seq, last), lambda b, h: (b, h, 0, 0))
    return pl.pallas_call(
        _attn_kernel,
        grid=(bsz, nh),
        in_specs=[head_spec(q.shape[-1]), head_spec(k.shape[-1]), head_spec(dv)],
        out_specs=pl.BlockSpec((1, seq, dv), lambda b, h: (b, 0, h)),
        out_shape=jax.ShapeDtypeStruct((bsz, seq, nh * dv), BF16),
        compiler_params=pltpu.CompilerParams(dimension_semantics=("arbitrary", "arbitrary")),
        name="causal_attention",
    )(q, k, v)


def _oproj_route_kernel(x_ref, o_ref, wo_ref, g_ref, wr_ref,
                        xo_ref, h_ref, idx_ref, gate_ref, rank_ref, count_ref, carry_ref):
    x = x_ref[...] + _dot(o_ref[...], wo_ref[...])
    xo_ref[...] = x
    hn = _rms(x, g_ref[...])
    _store_row_tiles(h_ref, hn)
    logits = lax.dot_general(wr_ref[...], hn.astype(BF16), (((1,), (1,)), ((), ())),
                             preferred_element_type=F32)[:N_EXPERTS]
    expert = lax.broadcasted_iota(jnp.int32, logits.shape, 0)
    e = jnp.exp(logits - jnp.max(logits, axis=0, keepdims=True))
    probs = e / jnp.sum(e, axis=0, keepdims=True)
    p1 = jnp.max(probs, axis=0, keepdims=True)
    i1 = jnp.min(jnp.where(probs == p1, expert, N_EXPERTS), axis=0, keepdims=True)
    rest = jnp.where(expert == i1, -1.0, probs)
    p2 = jnp.max(rest, axis=0, keepdims=True)
    i2 = jnp.min(jnp.where(rest == p2, expert, N_EXPERTS), axis=0, keepdims=True)
    denom = p1 + p2
    idx_ref[...] = jnp.concatenate([i1, i2], axis=0)
    gate_ref[...] = jnp.concatenate([p1 / denom, p2 / denom], axis=0)

    @pl.when(pl.program_id(0) == 0)
    def _():
        carry_ref[...] = jnp.zeros_like(carry_ref)

    tm = x_ref.shape[0]
    slot_row = lax.broadcasted_iota(jnp.int32, (carry_ref.shape[0], tm), 0)
    onehot = jnp.where((slot_row == i1) | (slot_row == i2), 1.0, 0.0)
    earlier = jnp.where(lax.broadcasted_iota(jnp.int32, (tm, tm), 0)
                        < lax.broadcasted_iota(jnp.int32, (tm, tm), 1), 1.0, 0.0).astype(BF16)
    before = _dot(onehot.astype(BF16), earlier) + carry_ref[:, 0:1]
    r1 = jnp.sum(jnp.where(slot_row == i1, before, 0.0), axis=0, keepdims=True)
    r2 = jnp.sum(jnp.where(slot_row == i2, before, 0.0), axis=0, keepdims=True)
    rank_ref[...] = jnp.concatenate([r1, r2], axis=0).astype(jnp.int32)
    total = carry_ref[...] + jnp.sum(onehot, axis=1, keepdims=True)
    carry_ref[...] = total
    count_ref[...] = total.astype(jnp.int32)


def _oproj_route(x2, o2, w_o, gain, router, layer):
    t, d = x2.shape
    assert d == V7X_SUBLANES * V7X_LANES, "row-DMA layout stores one token row per (8, 128) tile"
    tm = min(ROW_TILE, t)
    bf16_rows = 2 * V7X_SUBLANES
    wr = jnp.pad(router.T, ((0, bf16_rows - N_EXPERTS), (0, 0))).astype(BF16)
    return pl.pallas_call(
        _oproj_route_kernel,
        grid=(t // tm,),
        in_specs=[
            pl.BlockSpec((tm, d), lambda i: (i, 0)),
            pl.BlockSpec((tm, o2.shape[1]), lambda i: (i, 0)),
            _resident(w_o.shape, layer),
            _resident((1, d)),
            _resident(wr.shape),
        ],
        out_specs=[
            pl.BlockSpec((tm, d), lambda i: (i, 0)),
            pl.BlockSpec((tm * V7X_SUBLANES, V7X_LANES), lambda i: (i, 0)),
            pl.BlockSpec((TOP_K, tm), lambda i: (0, i)),
            pl.BlockSpec((TOP_K, tm), lambda i: (0, i)),
            pl.BlockSpec((TOP_K, tm), lambda i: (0, i)),
            pl.BlockSpec((bf16_rows, V7X_LANES), lambda i: (0, 0)),
        ],
        out_shape=[
            jax.ShapeDtypeStruct((t, d), F32),
            jax.ShapeDtypeStruct((t * V7X_SUBLANES, V7X_LANES), F32),
            jax.ShapeDtypeStruct((TOP_K, t), jnp.int32),
            jax.ShapeDtypeStruct((TOP_K, t), F32),
            jax.ShapeDtypeStruct((TOP_K, t), jnp.int32),
            jax.ShapeDtypeStruct((bf16_rows, V7X_LANES), jnp.int32),
        ],
        scratch_shapes=[pltpu.VMEM((bf16_rows, V7X_LANES), F32)],
        compiler_params=pltpu.CompilerParams(
            dimension_semantics=("arbitrary",), vmem_limit_bytes=BIG_VMEM_LIMIT),
        name="oproj_route",
    )(x2, o2, w_o, gain.reshape(1, d), wr)


def _load_row_tiles(tiles_ref):
    rows = tiles_ref.shape[0] // V7X_SUBLANES
    return jnp.concatenate(
        [tiles_ref[pl.ds(c, rows, stride=V7X_SUBLANES), :] for c in range(V7X_SUBLANES)], axis=-1)


def _store_row_tiles(tiles_ref, value):
    rows = value.shape[0]
    for c in range(V7X_SUBLANES):
        tiles_ref[pl.ds(c, rows, stride=V7X_SUBLANES), :] = value[:, c * V7X_LANES:(c + 1) * V7X_LANES]


def _row_tile(ref, row):
    return ref.at[pl.ds(pl.multiple_of(row * V7X_SUBLANES, V7X_SUBLANES), V7X_SUBLANES), :]


def _dispatch_copies(h_ref, xs_ref, pos_ref, sem, group):
    tokens = pos_ref.shape[0] // TOP_K
    base = pl.multiple_of(group * V7X_SUBLANES, V7X_SUBLANES)
    copies = []
    for u in range(V7X_SUBLANES):
        for k in range(TOP_K):
            dst_row = pos_ref[k * tokens + base + u]
            copies.append(pltpu.make_async_copy(
                _row_tile(h_ref, base + u), _row_tile(xs_ref, dst_row), sem))
    return copies


def _pad_tile_copies(meta_ref, zero_ref, xs_ref, sem):
    tile_sublanes = zero_ref.shape[0]
    tm = tile_sublanes // V7X_SUBLANES
    n_tiles = xs_ref.shape[0] // tile_sublanes

    def zero_tile(first_row):
        first = pl.multiple_of(first_row * V7X_SUBLANES, tile_sublanes)
        return pltpu.make_async_copy(zero_ref, xs_ref.at[pl.ds(first, tile_sublanes), :], sem)

    pairs = []
    for e in range(N_EXPERTS):
        end = meta_ref[e]
        pairs.append((end >= tm, zero_tile(jnp.maximum(end - tm, 0))))
    n_active = meta_ref[N_EXPERTS]
    for j in range(N_EXPERTS):
        tile = n_tiles - 1 - j
        pairs.append((tile >= n_active, zero_tile(tile * tm)))
    return pairs


def _dispatch_kernel(meta_ref, pos_ref, h_ref, xs_ref, zero_ref, zero_sem, sem):
    groups = h_ref.shape[0] // (V7X_SUBLANES * V7X_SUBLANES)

    @pl.when(pl.program_id(0) == 0)
    def _():
        zero_ref[...] = jnp.zeros_like(zero_ref)
        pairs = _pad_tile_copies(meta_ref, zero_ref, xs_ref, zero_sem)
        for cond, cp in pairs:
            pl.when(cond)(cp.start)
        for cond, cp in pairs:
            pl.when(cond)(cp.wait)

    def start(g, carry):
        for i, cp in enumerate(_dispatch_copies(h_ref, xs_ref, pos_ref, sem, g)):
            cp.start(priority=i % 2)
        return carry

    def wait(g, carry):
        for cp in _dispatch_copies(h_ref, xs_ref, pos_ref, sem, g):
            cp.wait()
        return carry

    lax.fori_loop(0, groups, start, 0)
    lax.fori_loop(0, groups, wait, 0)


def _dispatch(h3, pos_flat, meta, n_rows):
    t = h3.shape[0] // V7X_SUBLANES
    tm = min(MOVE_TILE, t)
    return pl.pallas_call(
        _dispatch_kernel,
        grid_spec=pltpu.PrefetchScalarGridSpec(
            num_scalar_prefetch=1,
            grid=(t // tm,),
            in_specs=[
                pl.BlockSpec((TOP_K * tm,), lambda i, meta: (i,), memory_space=pltpu.SMEM),
                pl.BlockSpec((tm * V7X_SUBLANES, V7X_LANES), lambda i, meta: (i, 0)),
            ],
            out_specs=pl.BlockSpec(memory_space=pl.ANY),
            scratch_shapes=[pltpu.VMEM((GROUP_TILE * V7X_SUBLANES, V7X_LANES), h3.dtype),
                            pltpu.SemaphoreType.DMA(()), pltpu.SemaphoreType.DMA(())],
        ),
        out_shape=jax.ShapeDtypeStruct((n_rows * V7X_SUBLANES, V7X_LANES), h3.dtype),
        compiler_params=pltpu.CompilerParams(dimension_semantics=("arbitrary",)),
        name="moe_dispatch",
    )(meta, pos_flat, h3)


def _experts_kernel(te_ref, meta_ref, x_ref, wg_ref, wu_ref, wd_ref, y_ref):
    del te_ref
    n_active = meta_ref[N_EXPERTS]

    @pl.when(pl.program_id(0) < n_active)
    def _():
        h = _load_row_tiles(x_ref).astype(BF16)
        gate = _dot(h, wg_ref[...])
        up = _dot(h, wu_ref[...])
        act = (gate * jax.nn.sigmoid(gate) * up).astype(BF16)
        _store_row_tiles(y_ref, _dot(act, wd_ref[...]))

    @pl.when(pl.program_id(0) >= n_active)
    def _():
        y_ref[...] = jnp.zeros_like(y_ref)


def _experts(xs, tile_expert, meta, wg, wu, wd, layer):
    tile_sublanes = GROUP_TILE * V7X_SUBLANES
    d, f = wg.shape[-2:]

    def row_map(i, te, meta):
        return (jnp.minimum(i, meta[N_EXPERTS] - 1), 0)

    def w_map(i, te, meta):
        return (layer, te[jnp.minimum(i, meta[N_EXPERTS] - 1)], 0, 0)

    return pl.pallas_call(
        _experts_kernel,
        grid_spec=pltpu.PrefetchScalarGridSpec(
            num_scalar_prefetch=2,
            grid=(xs.shape[0] // tile_sublanes,),
            in_specs=[
                pl.BlockSpec((tile_sublanes, V7X_LANES), row_map),
                pl.BlockSpec((None, None, d, f), w_map),
                pl.BlockSpec((None, None, d, f), w_map),
                pl.BlockSpec((None, None, f, d), w_map),
            ],
            out_specs=pl.BlockSpec((tile_sublanes, V7X_LANES), lambda i, te, meta: (i, 0)),
        ),
        out_shape=jax.ShapeDtypeStruct(xs.shape, F32),
        compiler_params=pltpu.CompilerParams(
            dimension_semantics=("arbitrary",), vmem_limit_bytes=BIG_VMEM_LIMIT),
        name="moe_experts",
    )(tile_expert, meta, xs, wg, wu, wd)


def _combine_copies(ys_ref, buf_ref, pos_ref, sem, group):
    tokens = pos_ref.shape[0] // TOP_K
    base = pl.multiple_of(group * V7X_SUBLANES, V7X_SUBLANES)
    copies = []
    for u in range(V7X_SUBLANES):
        for k in range(TOP_K):
            src_row = pos_ref[k * tokens + base + u]
            copies.append(pltpu.make_async_copy(
                _row_tile(ys_ref, src_row), _row_tile(buf_ref.at[k], base + u), sem))
    return copies


def _combine_kernel(pos_ref, x_ref, gate_ref, ys_ref, o_ref, buf_ref, sem):
    groups = x_ref.shape[0] // V7X_SUBLANES

    def start(g, carry):
        for i, cp in enumerate(_combine_copies(ys_ref, buf_ref, pos_ref, sem, g)):
            cp.start(priority=i % 2)
        return carry

    def wait(g, carry):
        for cp in _combine_copies(ys_ref, buf_ref, pos_ref, sem, g):
            cp.wait()
        return carry

    lax.fori_loop(0, groups, start, 0)
    lax.fori_loop(0, groups, wait, 0)
    gate = gate_ref[...]
    o_ref[...] = x_ref[...] + (gate[:, 0:1] * _load_row_tiles(buf_ref.at[0])
                               + gate[:, 1:2] * _load_row_tiles(buf_ref.at[1]))


def _combine(x2, gate, ys, pos_flat):
    t, d = x2.shape
    tm = min(MOVE_TILE, t)
    return pl.pallas_call(
        _combine_kernel,
        grid=(t // tm,),
        in_specs=[
            pl.BlockSpec((TOP_K * tm,), lambda i: (i,), memory_space=pltpu.SMEM),
            pl.BlockSpec((tm, d), lambda i: (i, 0)),
            pl.BlockSpec((tm, TOP_K), lambda i: (i, 0)),
            pl.BlockSpec(memory_space=pl.ANY),
        ],
        out_specs=pl.BlockSpec((tm, d), lambda i: (i, 0)),
        out_shape=jax.ShapeDtypeStruct((t, d), F32),
        scratch_shapes=[pltpu.VMEM((TOP_K, tm * V7X_SUBLANES, V7X_LANES), F32),
                        pltpu.SemaphoreType.DMA(())],
        compiler_params=pltpu.CompilerParams(dimension_semantics=("arbitrary",)),
        name="moe_combine",
    )(pos_flat, x2, gate, ys)


def _moe(x2, h3, idx, gate, rank, counts, wg, wu, wd, layer):
    t = x2.shape[0]
    tm = GROUP_TILE
    counts = counts[:N_EXPERTS, 0]
    padded = (counts + tm - 1) // tm * tm
    ends = jnp.cumsum(padded)
    starts = ends - padded
    pos = starts[idx] + rank
    n_tiles = TOP_K * t // tm + N_EXPERTS
    tile_start = jnp.arange(n_tiles, dtype=jnp.int32) * tm
    tile_expert = jnp.minimum(
        jnp.sum((tile_start[:, None] >= ends[None, :]).astype(jnp.int32), axis=1), N_EXPERTS - 1)
    meta = jnp.concatenate([ends, ends[-1:] // tm]).astype(jnp.int32)
    move = min(MOVE_TILE, t)
    pos_blocks = pos.astype(jnp.int32).reshape(TOP_K, t // move, move).transpose(1, 0, 2).reshape(-1)
    xs = _dispatch(h3, pos_blocks, meta, n_tiles * tm)
    ys = _experts(xs, tile_expert.astype(jnp.int32), meta, wg, wu, wd, layer)
    return _combine(x2, gate.T, ys, pos_blocks)


def kernel(x, positions, norm_mix, norm_ffn, conv_w_in, conv_w, conv_w_out, mla_w_down, mla_q_a_norm, mla_kv_a_norm, mla_w_uq, mla_w_ukv, mla_q_norm, mla_k_norm, mla_w_o, ffn_w_gate, ffn_w_up, ffn_w_down, moe_router, moe_w_gate, moe_w_up, moe_w_down):
    bsz, seq, d = x.shape
    depth = norm_mix.shape[0]
    cos_t, sinm_t = _rope_tables(positions)
    conv_w_in, conv_w_out, mla_w_o = (w.astype(BF16) for w in (conv_w_in, conv_w_out, mla_w_o))
    ffn_w_gate, ffn_w_up, ffn_w_down = (w.astype(BF16) for w in (ffn_w_gate, ffn_w_up, ffn_w_down))
    moe_w_gate, moe_w_up, moe_w_down = (w.astype(BF16) for w in (moe_w_gate, moe_w_up, moe_w_down))
    for i in range(depth):
        j = i // 2
        if i % 2 == 0:
            x = _conv_mixer(x, norm_mix[i], conv_w_in, conv_w, conv_w_out, j)
            x = _dense_ffn(x.reshape(bsz * seq, d), norm_ffn[i], ffn_w_gate, ffn_w_up, ffn_w_down,
                           j).reshape(bsz, seq, d)
        else:
            q, k, v = _mla_proj(x, norm_mix[i], mla_w_down[j], mla_q_a_norm[j], mla_kv_a_norm[j],
                                mla_w_uq[j], mla_w_ukv[j], mla_q_norm[j], mla_k_norm[j], cos_t, sinm_t)
            o = _attention(q, k, v)
            x2, h3, idx, gate, rank, counts = _oproj_route(
                x.reshape(bsz * seq, d), o.reshape(bsz * seq, -1), mla_w_o, norm_ffn[i], moe_router[j], j)
            x = _moe(x2, h3, idx, gate, rank, counts, moe_w_gate, moe_w_up, moe_w_down,
                     j).reshape(bsz, seq, d)
    return x
```

```python
import jax
import jax.numpy as jnp
from jax import lax
from jax.experimental import pallas as pl
from jax.experimental.pallas import tpu as pltpu

N_HEADS = 8
QK_NOPE_DIM = 128
QK_ROPE_DIM = 64
QK_HEAD_DIM = QK_NOPE_DIM + QK_ROPE_DIM
V_HEAD_DIM = 128
Q_LORA_RANK = 384
KV_LORA_RANK = 256
ROPE_THETA = 10000.0
N_EXPERTS = 8
TOP_K = 2
EPS = 1e-6

V7X_LANES = 128
V7X_SUBLANES = 8
V7X_VMEM_BYTES = 64 * 1024 * 1024
BIG_VMEM_LIMIT = V7X_VMEM_BYTES - 8 * 1024 * 1024

QK_PAD_DIM = 2 * V7X_LANES
Q_HEAD_COLS = 3 * V7X_LANES
MASK_VALUE = -1e30

ROW_TILE = 512
SUB_TILE = 256
ATTN_TILE = 512
GROUP_TILE = 512
MOVE_TILE = 512

F32 = jnp.float32
BF16 = jnp.bfloat16


def _rms(x, gain):
    return x * lax.rsqrt(jnp.mean(x * x, axis=-1, keepdims=True) + EPS) * gain


def _dot(a, b):
    return jnp.dot(a, b, preferred_element_type=F32)


def _resident(shape, layer=None):
    nd = len(shape)
    if layer is None:
        return pl.BlockSpec(shape, lambda *_: (0,) * nd, pipeline_mode=pl.Buffered(1))
    return pl.BlockSpec((None,) + tuple(shape[1:]), lambda *_: (layer,) + (0,) * (nd - 1),
                        pipeline_mode=pl.Buffered(1))


def _conv_kernel(x_ref, g_ref, w_in_ref, cw_ref, w_out_ref, o_ref, carry_ref):
    d = x_ref.shape[-1]
    tm = x_ref.shape[1]

    @pl.when(pl.program_id(1) == 0)
    def _():
        carry_ref[...] = jnp.zeros_like(carry_ref)

    x = x_ref[0]
    h = _rms(x, g_ref[...]).astype(BF16)
    y = _dot(h, w_in_ref[...])
    b_gate = y[:, :d]
    v = y[:, d:2 * d] * y[:, 2 * d:]
    prev = carry_ref[...]
    p1 = prev[V7X_SUBLANES - 1:V7X_SUBLANES]
    p2 = prev[V7X_SUBLANES - 2:V7X_SUBLANES - 1]
    rows = lax.broadcasted_iota(jnp.int32, (tm, d), 0)
    v1 = jnp.where(rows == 0, p1, pltpu.roll(v, 1, 0))
    v2 = jnp.where(rows == 0, p2, jnp.where(rows == 1, p1, pltpu.roll(v, 2, 0)))
    carry_ref[...] = v[tm - V7X_SUBLANES:]
    cw = cw_ref[...]
    conv = v2 * cw[0:1] + v1 * cw[1:2] + v * cw[2:3]
    z = (b_gate * conv).astype(BF16)
    o_ref[0] = x + _dot(z, w_out_ref[...])


def _conv_mixer(x, gain, w_in, conv_w, w_out, layer):
    bsz, seq, d = x.shape
    tm = min(ROW_TILE, seq)
    return pl.pallas_call(
        _conv_kernel,
        grid=(bsz, seq // tm),
        in_specs=[
            pl.BlockSpec((1, tm, d), lambda b, s: (b, s, 0)),
            _resident((1, d)),
            _resident(w_in.shape, layer),
            _resident(conv_w.shape, layer),
            _resident(w_out.shape, layer),
        ],
        out_specs=pl.BlockSpec((1, tm, d), lambda b, s: (b, s, 0)),
        out_shape=jax.ShapeDtypeStruct(x.shape, F32),
        scratch_shapes=[pltpu.VMEM((V7X_SUBLANES, d), F32)],
        compiler_params=pltpu.CompilerParams(
            dimension_semantics=("arbitrary", "arbitrary"), vmem_limit_bytes=BIG_VMEM_LIMIT),
        name="conv_mixer",
    )(x, gain.reshape(1, d), w_in, conv_w, w_out)


def _ffn_kernel(x_ref, g_ref, wg_ref, wu_ref, wd_ref, o_ref):
    x = x_ref[...]
    h = _rms(x, g_ref[...]).astype(BF16)
    gate = _dot(h, wg_ref[...])
    up = _dot(h, wu_ref[...])
    act = (gate * jax.nn.sigmoid(gate) * up).astype(BF16)
    o_ref[...] = x + _dot(act, wd_ref[...])


def _dense_ffn(x2, gain, wg, wu, wd, layer):
    t, d = x2.shape
    tm = min(ROW_TILE, t)
    return pl.pallas_call(
        _ffn_kernel,
        grid=(t // tm,),
        in_specs=[
            pl.BlockSpec((tm, d), lambda i: (i, 0)),
            _resident((1, d)),
            _resident(wg.shape, layer),
            _resident(wu.shape, layer),
            _resident(wd.shape, layer),
        ],
        out_specs=pl.BlockSpec((tm, d), lambda i: (i, 0)),
        out_shape=jax.ShapeDtypeStruct(x2.shape, F32),
        compiler_params=pltpu.CompilerParams(
            dimension_semantics=("arbitrary",), vmem_limit_bytes=BIG_VMEM_LIMIT),
        name="dense_ffn",
    )(x2, gain.reshape(1, d), wg, wu, wd)


def _rope_table_kernel(pos_ref, inv_ref, cos_ref, sin_ref, nsin_ref):
    ang = pos_ref[...] * inv_ref[...]
    s = jnp.sin(ang)
    cos_ref[...] = jnp.cos(ang)
    sin_ref[...] = s
    nsin_ref[...] = -s


def _rope_tables(positions):
    bsz, seq = positions.shape
    half = QK_ROPE_DIM // 2
    t = bsz * seq
    inv_freq = ROPE_THETA ** (-jnp.arange(0, QK_ROPE_DIM, 2, dtype=F32) / QK_ROPE_DIM)
    per_row = V7X_LANES // half
    rows = t // per_row
    pos_rep = jnp.repeat(positions.reshape(-1).astype(F32), half).reshape(rows, V7X_LANES)
    inv_rep = jnp.tile(inv_freq, per_row).reshape(1, V7X_LANES)
    tr = min(1024, rows)
    shp = jax.ShapeDtypeStruct((rows, V7X_LANES), F32)
    cos, sin, nsin = pl.pallas_call(
        _rope_table_kernel,
        grid=(rows // tr,),
        in_specs=[pl.BlockSpec((tr, V7X_LANES), lambda i: (i, 0)),
                  pl.BlockSpec((1, V7X_LANES), lambda i: (0, 0))],
        out_specs=[pl.BlockSpec((tr, V7X_LANES), lambda i: (i, 0))] * 3,
        out_shape=[shp, shp, shp],
        name="rope_tables",
    )(pos_rep, inv_rep)
    cos, sin, nsin = (a.reshape(bsz, seq, half) for a in (cos, sin, nsin))
    zeros = jnp.zeros((bsz, seq, V7X_LANES - QK_ROPE_DIM), F32)
    cos_t = jnp.concatenate([cos, cos, zeros], axis=-1)
    sinm_t = jnp.concatenate([nsin, sin, zeros], axis=-1)
    return cos_t, sinm_t


def _mla_proj_kernel(x_ref, g_ref, wdn_ref, qa_ref, kva_ref, wuq_ref, wukv_ref,
                     qn_ref, kn_ref, cos_ref, sinm_ref, q_ref, k_ref, v_ref):
    qn = qn_ref[...]
    kn = kn_ref[...]
    scale = QK_HEAD_DIM ** -0.5
    tm = x_ref.shape[1]
    sub = min(SUB_TILE, tm)
    lane = lax.broadcasted_iota(jnp.int32, (1, V7X_LANES), 1)
    rope_lanes = lane < QK_ROPE_DIM
    r = lax.broadcasted_iota(jnp.int32, (QK_PAD_DIM, QK_PAD_DIM), 0)
    c = lax.broadcasted_iota(jnp.int32, (QK_PAD_DIM, QK_PAD_DIM), 1)
    pair_ones = jnp.where((r < V7X_LANES) == (c < V7X_LANES), 1.0, 0.0).astype(BF16)

    def pair_sums(a, b):
        sums = _dot(jnp.concatenate([a, b], axis=-1).astype(BF16), pair_ones)
        return sums[:, :V7X_LANES], sums[:, V7X_LANES:]

    for r0 in range(0, tm, sub):
        rows = slice(r0, r0 + sub)
        h = _rms(x_ref[0, rows, :], g_ref[...]).astype(BF16)
        down = _dot(h, wdn_ref[...])
        c_q = _rms(down[:, :Q_LORA_RANK], qa_ref[...]).astype(BF16)
        c_kv = _rms(down[:, Q_LORA_RANK:Q_LORA_RANK + KV_LORA_RANK], kva_ref[...]).astype(BF16)
        k_rope = down[:, Q_LORA_RANK + KV_LORA_RANK:]
        q = _dot(c_q, wuq_ref[...])
        kv = _dot(c_kv, wukv_ref[...])
        cos = cos_ref[0, rows, :]
        sinm = sinm_ref[0, rows, :]
        kr = k_rope * kn[:, V7X_LANES:]
        kr_rot = kr * cos + pltpu.roll(kr, QK_ROPE_DIM, 1) * sinm
        kr_sq = jnp.where(rope_lanes, k_rope * k_rope, 0.0)
        for h0 in range(0, N_HEADS, 2):
            heads = (h0, h0 + 1)
            q3 = [q[:, hd * Q_HEAD_COLS:(hd + 1) * Q_HEAD_COLS] for hd in heads]
            k_nope = [kv[:, hd * QK_PAD_DIM:hd * QK_PAD_DIM + QK_NOPE_DIM] for hd in heads]
            q_ss = pair_sums(*[t[:, :V7X_LANES] * t[:, :V7X_LANES]
                               + t[:, V7X_LANES:QK_PAD_DIM] * t[:, V7X_LANES:QK_PAD_DIM] for t in q3])
            k_ss = pair_sums(*[t * t + kr_sq for t in k_nope])
            for i, hd in enumerate(heads):
                inv_q = lax.rsqrt(q_ss[i] / QK_HEAD_DIM + EPS) * scale
                qh = [q3[i][:, j * V7X_LANES:(j + 1) * V7X_LANES] * inv_q
                      * qn[:, j * V7X_LANES:(j + 1) * V7X_LANES] for j in range(3)]
                q_ref[0, hd, rows, :V7X_LANES] = qh[0].astype(BF16)
                q_ref[0, hd, rows, V7X_LANES:] = (qh[1] * cos + qh[2] * sinm).astype(BF16)
                inv_k = lax.rsqrt(k_ss[i] / QK_HEAD_DIM + EPS)
                k_ref[0, hd, rows, :V7X_LANES] = (k_nope[i] * inv_k * kn[:, :V7X_LANES]).astype(BF16)
                k_ref[0, hd, rows, V7X_LANES:] = (kr_rot * inv_k).astype(BF16)
                v_ref[0, hd, rows, :] = kv[:, hd * QK_PAD_DIM + QK_NOPE_DIM:(hd + 1) * QK_PAD_DIM].astype(BF16)


def _swap_halves(a):
    half = QK_ROPE_DIM // 2
    return jnp.concatenate([a[..., half:], a[..., :half]], axis=-1)


def _mla_proj(x, gain, w_down, q_a_norm, kv_a_norm, w_uq, w_ukv, q_norm, k_norm, cos_t, sinm_t):
    bsz, seq, d = x.shape
    tm = min(ROW_TILE, seq)
    lane_pad = V7X_LANES - QK_ROPE_DIM
    w_rope = w_down[:, Q_LORA_RANK + KV_LORA_RANK:]
    wdn = jnp.concatenate([w_down, _swap_halves(w_rope)], axis=1).astype(BF16)
    wuq = w_uq.reshape(Q_LORA_RANK, N_HEADS, QK_HEAD_DIM)
    wq_rope = wuq[..., QK_NOPE_DIM:]
    zeros = jnp.zeros((Q_LORA_RANK, N_HEADS, lane_pad), w_uq.dtype)
    wuq = jnp.concatenate([wuq[..., :QK_NOPE_DIM], wq_rope, zeros, _swap_halves(wq_rope), zeros], axis=-1)
    wuq = wuq.reshape(Q_LORA_RANK, N_HEADS * Q_HEAD_COLS).astype(BF16)
    wukv = w_ukv.astype(BF16)
    gz = jnp.zeros((lane_pad,), F32)
    qn = jnp.concatenate([q_norm[:QK_NOPE_DIM], q_norm[QK_NOPE_DIM:], gz,
                          _swap_halves(q_norm[QK_NOPE_DIM:]), gz]).reshape(1, Q_HEAD_COLS)
    kn = jnp.concatenate([k_norm, _swap_halves(k_norm[QK_NOPE_DIM:])]).reshape(1, QK_PAD_DIM)
    row_spec = pl.BlockSpec((1, tm, d), lambda b, s: (b, s, 0))
    tab_spec = pl.BlockSpec((1, tm, V7X_LANES), lambda b, s: (b, s, 0))
    qk_spec = pl.BlockSpec((1, N_HEADS, tm, QK_PAD_DIM), lambda b, s: (b, 0, s, 0))
    v_spec = pl.BlockSpec((1, N_HEADS, tm, V_HEAD_DIM), lambda b, s: (b, 0, s, 0))
    qk_shape = jax.ShapeDtypeStruct((bsz, N_HEADS, seq, QK_PAD_DIM), BF16)
    v_shape = jax.ShapeDtypeStruct((bsz, N_HEADS, seq, V_HEAD_DIM), BF16)
    return pl.pallas_call(
        _mla_proj_kernel,
        grid=(bsz, seq // tm),
        in_specs=[
            row_spec,
            _resident((1, d)),
            _resident(wdn.shape),
            _resident((1, Q_LORA_RANK)),
            _resident((1, KV_LORA_RANK)),
            _resident(wuq.shape),
            _resident(wukv.shape),
            _resident(qn.shape),
            _resident(kn.shape),
            tab_spec,
            tab_spec,
        ],
        out_specs=[qk_spec, qk_spec, v_spec],
        out_shape=[qk_shape, qk_shape, v_shape],
        compiler_params=pltpu.CompilerParams(
            dimension_semantics=("arbitrary", "arbitrary"), vmem_limit_bytes=BIG_VMEM_LIMIT),
        name="mla_proj",
    )(x, gain.reshape(1, d), wdn, q_a_norm.reshape(1, -1), kv_a_norm.reshape(1, -1), wuq, wukv,
      qn, kn, cos_t, sinm_t)


def _attn_kernel(q_ref, k_ref, v_ref, o_ref):
    seq = q_ref.shape[2]
    dv = v_ref.shape[3]
    t = min(ATTN_TILE, seq)
    dims = (((1,), (1,)), ((), ()))
    row = lax.broadcasted_iota(jnp.int32, (t, t), 0)
    col = lax.broadcasted_iota(jnp.int32, (t, t), 1)
    for qi in range(seq // t):
        q = q_ref[0, 0, qi * t:(qi + 1) * t, :]
        m = acc = None
        for ki in range(qi + 1):
            k = k_ref[0, 0, ki * t:(ki + 1) * t, :]
            v = v_ref[0, 0, ki * t:(ki + 1) * t, :]
            v_ext = jnp.concatenate([v, jnp.ones_like(v)], axis=-1)
            s = lax.dot_general(q, k, dims, preferred_element_type=F32)
            if ki == qi:
                s = jnp.where(col <= row, s, MASK_VALUE)
            s_max = jnp.max(s, axis=-1, keepdims=True)
            if m is None:
                m = s_max
                acc = _dot(jnp.exp(s - m).astype(BF16), v_ext)
            else:
                m_new = jnp.maximum(m, s_max)
                acc = jnp.exp(m - m_new) * acc + _dot(jnp.exp(s - m_new).astype(BF16), v_ext)
                m = m_new
        o_ref[0, qi * t:(qi + 1) * t, :] = (acc[:, :dv] / acc[:, dv:]).astype(o_ref.dtype)


def _attention(q, k, v):
    bsz, nh, seq, _ = q.shape
    dv = v.shape[-1]
    head_spec = lambda last: pl.BlockSpec((1, 1, seq, last), lambda b, h: (b, h, 0, 0))
    return pl.pallas_call(
        _attn_kernel,
        grid=(bsz, nh),
        in_specs=[head_spec(q.shape[-1]), head_spec(k.shape[-1]), head_spec(dv)],
        out_specs=pl.BlockSpec((1, seq, dv), lambda b, h: (b, 0, h)),
        out_shape=jax.ShapeDtypeStruct((bsz, seq, nh * dv), BF16),
        compiler_params=pltpu.CompilerParams(dimension_semantics=("arbitrary", "arbitrary")),
        name="causal_attention",
    )(q, k, v)


def _oproj_route_kernel(x_ref, o_ref, wo_ref, g_ref, wr_ref,
                        xo_ref, h_ref, idx_ref, gate_ref, rank_ref, count_ref, carry_ref):
    x = x_ref[...] + _dot(o_ref[...], wo_ref[...])
    xo_ref[...] = x
    hn = _rms(x, g_ref[...])
    _store_row_tiles(h_ref, hn)
    logits = lax.dot_general(wr_ref[...], hn.astype(BF16), (((1,), (1,)), ((), ())),
                             preferred_element_type=F32)[:N_EXPERTS]
    expert = lax.broadcasted_iota(jnp.int32, logits.shape, 0)
    e = jnp.exp(logits - jnp.max(logits, axis=0, keepdims=True))
    probs = e / jnp.sum(e, axis=0, keepdims=True)
    p1 = jnp.max(probs, axis=0, keepdims=True)
    i1 = jnp.min(jnp.where(probs == p1, expert, N_EXPERTS), axis=0, keepdims=True)
    rest = jnp.where(expert == i1, -1.0, probs)
    p2 = jnp.max(rest, axis=0, keepdims=True)
    i2 = jnp.min(jnp.where(rest == p2, expert, N_EXPERTS), axis=0, keepdims=True)
    denom = p1 + p2
    idx_ref[...] = jnp.concatenate([i1, i2], axis=0)
    gate_ref[...] = jnp.concatenate([p1 / denom, p2 / denom], axis=0)

    @pl.when(pl.program_id(0) == 0)
    def _():
        carry_ref[...] = jnp.zeros_like(carry_ref)

    tm = x_ref.shape[0]
    slot_row = lax.broadcasted_iota(jnp.int32, (carry_ref.shape[0], tm), 0)
    onehot = jnp.where((slot_row == i1) | (slot_row == i2), 1.0, 0.0)
    earlier = jnp.where(lax.broadcasted_iota(jnp.int32, (tm, tm), 0)
                        < lax.broadcasted_iota(jnp.int32, (tm, tm), 1), 1.0, 0.0).astype(BF16)
    before = _dot(onehot.astype(BF16), earlier) + carry_ref[:, 0:1]
    r1 = jnp.sum(jnp.where(slot_row == i1, before, 0.0), axis=0, keepdims=True)
    r2 = jnp.sum(jnp.where(slot_row == i2, before, 0.0), axis=0, keepdims=True)
    rank_ref[...] = jnp.concatenate([r1, r2], axis=0).astype(jnp.int32)
    total = carry_ref[...] + jnp.sum(onehot, axis=1, keepdims=True)
    carry_ref[...] = total
    count_ref[...] = total.astype(jnp.int32)


def _oproj_route(x2, o2, w_o, gain, router, layer):
    t, d = x2.shape
    assert d == V7X_SUBLANES * V7X_LANES, "row-DMA layout stores one token row per (8, 128) tile"
    tm = min(ROW_TILE, t)
    bf16_rows = 2 * V7X_SUBLANES
    wr = jnp.pad(router.T, ((0, bf16_rows - N_EXPERTS), (0, 0))).astype(BF16)
    return pl.pallas_call(
        _oproj_route_kernel,
        grid=(t // tm,),
        in_specs=[
            pl.BlockSpec((tm, d), lambda i: (i, 0)),
            pl.BlockSpec((tm, o2.shape[1]), lambda i: (i, 0)),
            _resident(w_o.shape, layer),
            _resident((1, d)),
            _resident(wr.shape),
        ],
        out_specs=[
            pl.BlockSpec((tm, d), lambda i: (i, 0)),
            pl.BlockSpec((tm * V7X_SUBLANES, V7X_LANES), lambda i: (i, 0)),
            pl.BlockSpec((TOP_K, tm), lambda i: (0, i)),
            pl.BlockSpec((TOP_K, tm), lambda i: (0, i)),
            pl.BlockSpec((TOP_K, tm), lambda i: (0, i)),
            pl.BlockSpec((bf16_rows, V7X_LANES), lambda i: (0, 0)),
        ],
        out_shape=[
            jax.ShapeDtypeStruct((t, d), F32),
            jax.ShapeDtypeStruct((t * V7X_SUBLANES, V7X_LANES), F32),
            jax.ShapeDtypeStruct((TOP_K, t), jnp.int32),
            jax.ShapeDtypeStruct((TOP_K, t), F32),
            jax.ShapeDtypeStruct((TOP_K, t), jnp.int32),
            jax.ShapeDtypeStruct((bf16_rows, V7X_LANES), jnp.int32),
        ],
        scratch_shapes=[pltpu.VMEM((bf16_rows, V7X_LANES), F32)],
        compiler_params=pltpu.CompilerParams(
            dimension_semantics=("arbitrary",), vmem_limit_bytes=BIG_VMEM_LIMIT),
        name="oproj_route",
    )(x2, o2, w_o, gain.reshape(1, d), wr)


def _load_row_tiles(tiles_ref):
    rows = tiles_ref.shape[0] // V7X_SUBLANES
    return jnp.concatenate(
        [tiles_ref[pl.ds(c, rows, stride=V7X_SUBLANES), :] for c in range(V7X_SUBLANES)], axis=-1)


def _store_row_tiles(tiles_ref, value):
    rows = value.shape[0]
    for c in range(V7X_SUBLANES):
        tiles_ref[pl.ds(c, rows, stride=V7X_SUBLANES), :] = value[:, c * V7X_LANES:(c + 1) * V7X_LANES]


def _row_tile(ref, row):
    return ref.at[pl.ds(pl.multiple_of(row * V7X_SUBLANES, V7X_SUBLANES), V7X_SUBLANES), :]


def _dispatch_copies(h_ref, xs_ref, pos_ref, sem, group):
    tokens = pos_ref.shape[0] // TOP_K
    base = pl.multiple_of(group * V7X_SUBLANES, V7X_SUBLANES)
    copies = []
    for u in range(V7X_SUBLANES):
        for k in range(TOP_K):
            dst_row = pos_ref[k * tokens + base + u]
            copies.append(pltpu.make_async_copy(
                _row_tile(h_ref, base + u), _row_tile(xs_ref, dst_row), sem))
    return copies


def _pad_tile_copies(meta_ref, zero_ref, xs_ref, sem):
    tile_sublanes = zero_ref.shape[0]
    tm = tile_sublanes // V7X_SUBLANES
    n_tiles = xs_ref.shape[0] // tile_sublanes

    def zero_tile(first_row):
        first = pl.multiple_of(first_row * V7X_SUBLANES, tile_sublanes)
        return pltpu.make_async_copy(zero_ref, xs_ref.at[pl.ds(first, tile_sublanes), :], sem)

    pairs = []
    for e in range(N_EXPERTS):
        end = meta_ref[e]
        pairs.append((end >= tm, zero_tile(jnp.maximum(end - tm, 0))))
    n_active = meta_ref[N_EXPERTS]
    for j in range(N_EXPERTS):
        tile = n_tiles - 1 - j
        pairs.append((tile >= n_active, zero_tile(tile * tm)))
    return pairs


def _dispatch_kernel(meta_ref, pos_ref, h_ref, xs_ref, zero_ref, zero_sem, sem):
    groups = h_ref.shape[0] // (V7X_SUBLANES * V7X_SUBLANES)

    @pl.when(pl.program_id(0) == 0)
    def _():
        zero_ref[...] = jnp.zeros_like(zero_ref)
        pairs = _pad_tile_copies(meta_ref, zero_ref, xs_ref, zero_sem)
        for cond, cp in pairs:
            pl.when(cond)(cp.start)
        for cond, cp in pairs:
            pl.when(cond)(cp.wait)

    def start(g, carry):
        for i, cp in enumerate(_dispatch_copies(h_ref, xs_ref, pos_ref, sem, g)):
            cp.start(priority=i % 2)
        return carry

    def wait(g, carry):
        for cp in _dispatch_copies(h_ref, xs_ref, pos_ref, sem, g):
            cp.wait()
        return carry

    lax.fori_loop(0, groups, start, 0)
    lax.fori_loop(0, groups, wait, 0)


def _dispatch(h3, pos_flat, meta, n_rows):
    t = h3.shape[0] // V7X_SUBLANES
    tm = min(MOVE_TILE, t)
    return pl.pallas_call(
        _dispatch_kernel,
        grid_spec=pltpu.PrefetchScalarGridSpec(
            num_scalar_prefetch=1,
            grid=(t // tm,),
            in_specs=[
                pl.BlockSpec((TOP_K * tm,), lambda i, meta: (i,), memory_space=pltpu.SMEM),
                pl.BlockSpec((tm * V7X_SUBLANES, V7X_LANES), lambda i, meta: (i, 0)),
            ],
            out_specs=pl.BlockSpec(memory_space=pl.ANY),
            scratch_shapes=[pltpu.VMEM((GROUP_TILE * V7X_SUBLANES, V7X_LANES), h3.dtype),
                            pltpu.SemaphoreType.DMA(()), pltpu.SemaphoreType.DMA(())],
        ),
        out_shape=jax.ShapeDtypeStruct((n_rows * V7X_SUBLANES, V7X_LANES), h3.dtype),
        compiler_params=pltpu.CompilerParams(dimension_semantics=("arbitrary",)),
        name="moe_dispatch",
    )(meta, pos_flat, h3)


def _experts_kernel(te_ref, meta_ref, x_ref, wg_ref, wu_ref, wd_ref, y_ref):
    del te_ref
    n_active = meta_ref[N_EXPERTS]

    @pl.when(pl.program_id(0) < n_active)
    def _():
        h = _load_row_tiles(x_ref).astype(BF16)
        gate = _dot(h, wg_ref[...])
        up = _dot(h, wu_ref[...])
        act = (gate * jax.nn.sigmoid(gate) * up).astype(BF16)
        _store_row_tiles(y_ref, _dot(act, wd_ref[...]))

    @pl.when(pl.program_id(0) >= n_active)
    def _():
        y_ref[...] = jnp.zeros_like(y_ref)


def _experts(xs, tile_expert, meta, wg, wu, wd, layer):
    tile_sublanes = GROUP_TILE * V7X_SUBLANES
    d, f = wg.shape[-2:]

    def row_map(i, te, meta):
        return (jnp.minimum(i, meta[N_EXPERTS] - 1), 0)

    def w_map(i, te, meta):
        return (layer, te[jnp.minimum(i, meta[N_EXPERTS] - 1)], 0, 0)

    return pl.pallas_call(
        _experts_kernel,
        grid_spec=pltpu.PrefetchScalarGridSpec(
            num_scalar_prefetch=2,
            grid=(xs.shape[0] // tile_sublanes,),
            in_specs=[
                pl.BlockSpec((tile_sublanes, V7X_LANES), row_map),
                pl.BlockSpec((None, None, d, f), w_map),
                pl.BlockSpec((None, None, d, f), w_map),
                pl.BlockSpec((None, None, f, d), w_map),
            ],
            out_specs=pl.BlockSpec((tile_sublanes, V7X_LANES), lambda i, te, meta: (i, 0)),
        ),
        out_shape=jax.ShapeDtypeStruct(xs.shape, F32),
        compiler_params=pltpu.CompilerParams(
            dimension_semantics=("arbitrary",), vmem_limit_bytes=BIG_VMEM_LIMIT),
        name="moe_experts",
    )(tile_expert, meta, xs, wg, wu, wd)


def _combine_copies(ys_ref, buf_ref, pos_ref, sem, group):
    tokens = pos_ref.shape[0] // TOP_K
    base = pl.multiple_of(group * V7X_SUBLANES, V7X_SUBLANES)
    copies = []
    for u in range(V7X_SUBLANES):
        for k in range(TOP_K):
            src_row = pos_ref[k * tokens + base + u]
            copies.append(pltpu.make_async_copy(
                _row_tile(ys_ref, src_row), _row_tile(buf_ref.at[k], base + u), sem))
    return copies


def _combine_kernel(pos_ref, x_ref, gate_ref, ys_ref, o_ref, buf_ref, sem):
    groups = x_ref.shape[0] // V7X_SUBLANES

    def start(g, carry):
        for i, cp in enumerate(_combine_copies(ys_ref, buf_ref, pos_ref, sem, g)):
            cp.start(priority=i % 2)
        return carry

    def wait(g, carry):
        for cp in _combine_copies(ys_ref, buf_ref, pos_ref, sem, g):
            cp.wait()
        return carry

    lax.fori_loop(0, groups, start, 0)
    lax.fori_loop(0, groups, wait, 0)
    gate = gate_ref[...]
    o_ref[...] = x_ref[...] + (gate[:, 0:1] * _load_row_tiles(buf_ref.at[0])
                               + gate[:, 1:2] * _load_row_tiles(buf_ref.at[1]))


def _combine(x2, gate, ys, pos_flat):
    t, d = x2.shape
    tm = min(MOVE_TILE, t)
    return pl.pallas_call(
        _combine_kernel,
        grid=(t // tm,),
        in_specs=[
            pl.BlockSpec((TOP_K * tm,), lambda i: (i,), memory_space=pltpu.SMEM),
            pl.BlockSpec((tm, d), lambda i: (i, 0)),
            pl.BlockSpec((tm, TOP_K), lambda i: (i, 0)),
            pl.BlockSpec(memory_space=pl.ANY),
        ],
        out_specs=pl.BlockSpec((tm, d), lambda i: (i, 0)),
        out_shape=jax.ShapeDtypeStruct((t, d), F32),
        scratch_shapes=[pltpu.VMEM((TOP_K, tm * V7X_SUBLANES, V7X_LANES), F32),
                        pltpu.SemaphoreType.DMA(())],
        compiler_params=pltpu.CompilerParams(dimension_semantics=("arbitrary",)),
        name="moe_combine",
    )(pos_flat, x2, gate, ys)


def _moe(x2, h3, idx, gate, rank, counts, wg, wu, wd, layer):
    t = x2.shape[0]
    tm = GROUP_TILE
    counts = counts[:N_EXPERTS, 0]
    padded = (counts + tm - 1) // tm * tm
    ends = jnp.cumsum(padded)
    starts = ends - padded
    group_start = sum(jnp.where(idx == e, starts[e], 0) for e in range(N_EXPERTS))
    pos = group_start + rank
    n_tiles = TOP_K * t // tm + N_EXPERTS
    tile_start = jnp.arange(n_tiles, dtype=jnp.int32) * tm
    tile_expert = jnp.minimum(
        jnp.sum((tile_start[:, None] >= ends[None, :]).astype(jnp.int32), axis=1), N_EXPERTS - 1)
    meta = jnp.concatenate([ends, ends[-1:] // tm]).astype(jnp.int32)
    move = min(MOVE_TILE, t)
    pos_blocks = pos.astype(jnp.int32).reshape(TOP_K, t // move, move).transpose(1, 0, 2).reshape(-1)
    xs = _dispatch(h3, pos_blocks, meta, n_tiles * tm)
    ys = _experts(xs, tile_expert.astype(jnp.int32), meta, wg, wu, wd, layer)
    return _combine(x2, gate.T, ys, pos_blocks)


def kernel(x, positions, norm_mix, norm_ffn, conv_w_in, conv_w, conv_w_out, mla_w_down, mla_q_a_norm, mla_kv_a_norm, mla_w_uq, mla_w_ukv, mla_q_norm, mla_k_norm, mla_w_o, ffn_w_gate, ffn_w_up, ffn_w_down, moe_router, moe_w_gate, moe_w_up, moe_w_down):
    bsz, seq, d = x.shape
    depth = norm_mix.shape[0]
    cos_t, sinm_t = _rope_tables(positions)
    conv_w_in, conv_w_out, mla_w_o = (w.astype(BF16) for w in (conv_w_in, conv_w_out, mla_w_o))
    ffn_w_gate, ffn_w_up, ffn_w_down = (w.astype(BF16) for w in (ffn_w_gate, ffn_w_up, ffn_w_down))
    moe_w_gate, moe_w_up, moe_w_down = (w.astype(BF16) for w in (moe_w_gate, moe_w_up, moe_w_down))
    for i in range(depth):
        j = i // 2
        if i % 2 == 0:
            x = _conv_mixer(x, norm_mix[i], conv_w_in, conv_w, conv_w_out, j)
            x = _dense_ffn(x.reshape(bsz * seq, d), norm_ffn[i], ffn_w_gate, ffn_w_up, ffn_w_down,
                           j).reshape(bsz, seq, d)
        else:
            q, k, v = _mla_proj(x, norm_mix[i], mla_w_down[j], mla_q_a_norm[j], mla_kv_a_norm[j],
                                mla_w_uq[j], mla_w_ukv[j], mla_q_norm[j], mla_k_norm[j], cos_t, sinm_t)
            o = _attention(q, k, v)
            x2, h3, idx, gate, rank, counts = _oproj_route(
                x.reshape(bsz * seq, d), o.reshape(bsz * seq, -1), mla_w_o, norm_ffn[i], moe_router[j], j)
            x = _moe(x2, h3, idx, gate, rank, counts, moe_w_gate, moe_w_up, moe_w_down,
                     j).reshape(bsz, seq, d)
    return x
```
